```python
import math
import jax, jax.numpy as jnp
from jax import lax
import numpy as np

D_MODEL = 1024
BATCH = 2
SEQ = 8192
DEPTH = 4
DEC_BATCH = 128
DEC_SEQ = 8
PAST_LEN = 2048
PAGE_SIZE = 128

N_EVEN = (DEPTH + 1) // 2
N_ODD = DEPTH // 2
FOX_HEADS = 8
FOX_HEAD_DIM = 64
FOX_W = FOX_HEADS * FOX_HEAD_DIM
FOX_BLOCK = 128
FOX_GATE_OFFSET = 3.0
HG_HEADS = 4
HG_DK = 128
HG_DV = 128
HG_W = HG_HEADS * HG_DK
HG_CHUNK = 64
IN_EVEN_SIZES = (FOX_W, FOX_W, FOX_W, FOX_HEADS, HG_W, HG_W, HG_W, HG_W)
IN_EVEN = sum(IN_EVEN_SIZES)
S5_GROUP = 16
S5_GROUPS = D_MODEL // S5_GROUP
S5_STATE = 64
PEER_HEADS = 8
PEER_NKEYS = 128
PEER_EXPERTS = PEER_NKEYS * PEER_NKEYS
PEER_QDIM = 256
PEER_TOPK = 16
PEER_BLOCK = 256
N_ADA = 6
EPS = 1e-6
MASK_VALUE = -1e30
F32 = jnp.float32

kernel_name = 'fox_hgrn2_s5_peer_adaln_decode_step'


def _rms_norm(x, w):
    xf = x.astype(F32)
    y = xf * lax.rsqrt(jnp.mean(xf * xf, axis=-1, keepdims=True) + EPS)
    return (y * w.astype(F32)).astype(x.dtype)


def _modulate(x, w, shift, scale):
    return _rms_norm(x, w) * (1 + scale[:, None, :]) + shift[:, None, :]


def _fox_attend(q, k, v, f_q, f_k, q_pos, k_pos):
    b, tq, nh, dh = q.shape
    blk = FOX_BLOCK if tq % FOX_BLOCK == 0 else tq
    nb = tq // blk
    scale = dh ** -0.5
    f_kT = jnp.transpose(f_k, (0, 2, 1))

    def one_block(args):
        qb, fqb, pb = args
        s = jnp.einsum('bqhd,bkhd->bhqk', qb, k).astype(F32) * scale
        s = s + jnp.transpose(fqb, (0, 2, 1))[..., None] - f_kT[:, :, None, :]
        s = jnp.where((k_pos[None, :] <= pb[:, None])[None, None], s, MASK_VALUE)
        p = jax.nn.softmax(s, axis=-1).astype(v.dtype)
        return jnp.einsum('bhqk,bkhd->bqhd', p, v)

    qs = q.reshape(b, nb, blk, nh, dh).transpose(1, 0, 2, 3, 4)
    fqs = f_q.reshape(b, nb, blk, nh).transpose(1, 0, 2, 3)
    ps = q_pos.reshape(nb, blk)
    o = lax.map(one_block, (qs, fqs, ps))
    return o.transpose(1, 0, 2, 3, 4).reshape(b, tq, nh, dh)


def _hgrn2_recurrence(q, k, v, logg, s0):
    b, t, nh, dk = q.shape
    L = min(HG_CHUNK, t)
    nc = -(-t // L)
    pad = nc * L - t

    def prep(a):
        a = jnp.pad(a.astype(F32), ((0, 0), (0, pad), (0, 0), (0, 0)))
        return a.reshape(b, nc, L, nh, a.shape[-1]).transpose(1, 0, 3, 2, 4)

    qc, kc, vc, gc = prep(q), prep(k), prep(v), prep(logg)
    tri = jnp.tril(jnp.ones((L, L), dtype=bool))[None, None, :, :, None]

    def step(S, inp):
        qb, kb, vb, gb = inp
        bc = jnp.cumsum(gb, axis=2)
        o_inter = jnp.einsum('bhtd,bhdv->bhtv', qb * jnp.exp(bc), S)
        diff = bc[:, :, :, None, :] - bc[:, :, None, :, :]
        decay = jnp.where(tri, jnp.exp(jnp.where(tri, diff, 0.0)), 0.0)
        att = jnp.einsum('bhtd,bhsd,bhtsd->bhts', qb, kb, decay)
        o = o_inter + jnp.einsum('bhts,bhsv->bhtv', att, vb)
        last = bc[:, :, -1:, :]
        S = jnp.exp(last[:, :, 0, :])[..., None] * S + jnp.einsum('bhsd,bhsv->bhdv', kb * jnp.exp(last - bc), vb)
        return S, o

    S, o = lax.scan(step, s0.astype(F32), (qc, kc, vc, gc))
    o = o.transpose(1, 0, 3, 2, 4).reshape(b, nc * L, nh, -1)[:, :t]
    return o, S


def _s5_discretize(a_re, a_im, b_re, b_im, log_step):
    dt = jnp.exp(log_step.astype(F32))[:, None]
    ar, ai = a_re.astype(F32), a_im.astype(F32)
    mag = jnp.exp(ar * dt)
    ang = ai * dt
    abar_re, abar_im = mag * jnp.cos(ang), mag * jnp.sin(ang)
    nr, ni = abar_re - 1.0, abar_im
    den = ar * ar + ai * ai
    cr = (nr * ar + ni * ai) / den
    ci = (ni * ar - nr * ai) / den
    br, bi = b_re.astype(F32), b_im.astype(F32)
    bbar_re = cr[..., None] * br - ci[..., None] * bi
    bbar_im = cr[..., None] * bi + ci[..., None] * br
    return abar_re, abar_im, bbar_re, bbar_im


def _s5_ssm(u, a_re, a_im, b_re, b_im, c_re, c_im, d_skip, log_step, x0):
    bsz, t, _ = u.shape
    abr, abi, bbr, bbi = _s5_discretize(a_re, a_im, b_re, b_im, log_step)
    ug = u.astype(F32).reshape(bsz, t, S5_GROUPS, S5_GROUP)
    bu_re = jnp.einsum('gnc,btgc->tbgn', bbr, ug)
    bu_im = jnp.einsum('gnc,btgc->tbgn', bbi, ug)
    if x0 is not None:
        x0r, x0i = x0[0].astype(F32), x0[1].astype(F32)
        bu_re = bu_re.at[0].add(abr * x0r - abi * x0i)
        bu_im = bu_im.at[0].add(abr * x0i + abi * x0r)
    ar_t = jnp.broadcast_to(abr, (t, 1, S5_GROUPS, S5_STATE))
    ai_t = jnp.broadcast_to(abi, (t, 1, S5_GROUPS, S5_STATE))

    def combine(e1, e2):
        a1r, a1i, b1r, b1i = e1
        a2r, a2i, b2r, b2i = e2
        return (a2r * a1r - a2i * a1i, a2r * a1i + a2i * a1r,
                a2r * b1r - a2i * b1i + b2r, a2r * b1i + a2i * b1r + b2i)

    _, _, xr, xi = lax.associative_scan(combine, (ar_t, ai_t, bu_re, bu_im), axis=0)
    y = jnp.einsum('gcn,tbgn->btgc', c_re.astype(F32), xr) - jnp.einsum('gcn,tbgn->btgc', c_im.astype(F32), xi)
    y = y + d_skip.astype(F32)[None, None] * ug
    return y.reshape(bsz, t, D_MODEL).astype(u.dtype), xr[-1], xi[-1]


def _peer(h, wq, sub_keys, u_tab, v_tab):
    bsz, t, d = h.shape
    xt = h.reshape(-1, d)
    n = xt.shape[0]
    blk = min(PEER_BLOCK, n)
    nb = -(-n // blk)
    xt = jnp.pad(xt, ((0, nb * blk - n), (0, 0)))

    def one(xb):
        q = (xb @ wq).reshape(blk, PEER_HEADS, 2, PEER_QDIM // 2)
        s = jnp.einsum('nhpd,pkd->nhpk', q, sub_keys).astype(F32)
        v1, i1 = lax.top_k(s[:, :, 0], PEER_TOPK)
        v2, i2 = lax.top_k(s[:, :, 1], PEER_TOPK)
        cand = (v1[..., :, None] + v2[..., None, :]).reshape(blk, PEER_HEADS, PEER_TOPK * PEER_TOPK)
        cidx = (i1[..., :, None] * PEER_NKEYS + i2[..., None, :]).reshape(blk, PEER_HEADS, PEER_TOPK * PEER_TOPK)
        sv, si = lax.top_k(cand, PEER_TOPK)
        eidx = jnp.take_along_axis(cidx, si, axis=-1)
        g = jax.nn.softmax(sv, axis=-1)
        act = jax.nn.gelu(jnp.einsum('nhkd,nd->nhk', u_tab[eidx], xb).astype(F32), approximate=False)
        return jnp.einsum('nhk,nhkd->nd', (g * act).astype(xb.dtype), v_tab[eidx])

    out = lax.map(one, xt.reshape(nb, blk, d))
    return out.reshape(-1, d)[:n].reshape(bsz, t, d)


def _even_mixer(h, e, lb, past, W):
    bsz, t, _ = h.shape
    proj = h @ W['w_in_even'][e]
    fq, fk, fv, ff, hq, hf, hi, hg = jnp.split(proj, list(np.cumsum(IN_EVEN_SIZES)[:-1]), axis=-1)
    q = _rms_norm(fq.reshape(bsz, t, FOX_HEADS, FOX_HEAD_DIM), W['q_norm_w'][e])
    k = _rms_norm(fk.reshape(bsz, t, FOX_HEADS, FOX_HEAD_DIM), W['k_norm_w'][e])
    v = fv.reshape(bsz, t, FOX_HEADS, FOX_HEAD_DIM)
    logf = jax.nn.log_sigmoid(ff.astype(F32) + W['fox_fbias'][e].astype(F32))
    cum_new = jnp.cumsum(logf, axis=1)
    if past is None:
        kk, vv, f_key = k, v, cum_new
        q_pos = jnp.arange(t, dtype=jnp.int32)
        k_pos = q_pos
        s0 = jnp.zeros((bsz, HG_HEADS, HG_DK, HG_DV), F32)
    else:
        pt = past['page_table']
        plen = pt.shape[1] * PAGE_SIZE
        kp = past['cache_k'][e, pt].reshape(bsz, plen, FOX_HEADS, FOX_HEAD_DIM).astype(k.dtype)
        vp = past['cache_v'][e, pt].reshape(bsz, plen, FOX_HEADS, FOX_HEAD_DIM).astype(v.dtype)
        lp = past['cache_logf'][e, pt].reshape(bsz, plen, FOX_HEADS).astype(F32)
        cum_p = jnp.cumsum(lp, axis=1)
        rev = cum_p[:, -1:] - cum_p
        kk = jnp.concatenate([kp, k], axis=1)
        vv = jnp.concatenate([vp, v], axis=1)
        f_key = jnp.concatenate([-rev, cum_new], axis=1)
        q_pos = plen + jnp.arange(t, dtype=jnp.int32)
        k_pos = jnp.arange(plen + t, dtype=jnp.int32)
        s0 = past['state_hgrn'][e]
    o_fox = _fox_attend(q, kk, vv, cum_new, f_key, q_pos, k_pos).reshape(bsz, t, FOX_W)
    lbh = lb.reshape(HG_HEADS, HG_DK)
    fgate = lbh + (1.0 - lbh) * jax.nn.sigmoid(hf.astype(F32).reshape(bsz, t, HG_HEADS, HG_DK))
    logg = jnp.log(fgate)
    k_in = 1.0 - fgate
    q_in = jax.nn.silu(hq).reshape(bsz, t, HG_HEADS, HG_DK)
    v_in = hi.reshape(bsz, t, HG_HEADS, HG_DV)
    o_h, s_new = _hgrn2_recurrence(q_in, k_in, v_in, logg, s0)
    o_h = _rms_norm(o_h, W['hgrn_onorm_w'][e]) * jax.nn.silu(hg.astype(F32).reshape(bsz, t, HG_HEADS, HG_DV))
    o_h = o_h.astype(h.dtype).reshape(bsz, t, HG_W)
    out = jnp.concatenate([o_fox, o_h], axis=-1) @ W['w_out_even'][e]
    return out, k, v, logf, s_new


def _odd_mixer(h, o, past, W):
    x0 = None if past is None else (past['state_ssm_re'][o], past['state_ssm_im'][o])
    y, sr, si = _s5_ssm(h, W['s5_A_re'][o], W['s5_A_im'][o], W['s5_B_re'][o], W['s5_B_im'][o],
                        W['s5_C_re'][o], W['s5_C_im'][o], W['s5_D'][o], W['s5_log_step'][o], x0)
    y = jax.nn.gelu(y, approximate=False)
    ab = y @ W['w_glu'][o] + W['b_glu'][o]
    a, b = jnp.split(ab, 2, axis=-1)
    return a * jax.nn.sigmoid(b), sr, si


def _trunk(x, c, past, W):
    lb_sm = jax.nn.softmax(W['hgrn_lb'].astype(F32), axis=0)
    lb_all = jnp.cumsum(lb_sm, axis=0) - lb_sm[0:1]
    ks, vs, lfs, hs, srs, sis = [], [], [], [], [], []
    for l in range(DEPTH):
        mod = c @ W['w_ada'][l] + W['b_ada'][l]
        sh1, sc1, g1, sh2, sc2, g2 = jnp.split(mod, N_ADA, axis=-1)
        h = _modulate(x, W['norm1_w'][l], sh1, sc1)
        if l % 2 == 0:
            e = l // 2
            mix, k_new, v_new, lf_new, s_new = _even_mixer(h, e, lb_all[e], past, W)
            ks.append(k_new)
            vs.append(v_new)
            lfs.append(lf_new)
            hs.append(s_new)
        else:
            mix, sr, si = _odd_mixer(h, l // 2, past, W)
            srs.append(sr)
            sis.append(si)
        x = x + g1[:, None, :] * mix
        h2 = _modulate(x, W['norm2_w'][l], sh2, sc2)
        x = x + g2[:, None, :] * _peer(h2, W['peer_wq'][l], W['peer_subkeys'][l], W['peer_u'][l], W['peer_v'][l])
    return x, jnp.stack(ks), jnp.stack(vs), jnp.stack(lfs), jnp.stack(hs), jnp.stack(srs), jnp.stack(sis)


def setup_inputs(seed: int = 0) -> dict:
    key = jax.random.key(seed)
    kit = iter(list(jax.random.split(key, 48)))

    def nrm(shape, s):
        return jax.random.normal(next(kit), shape, F32) * s

    d = D_MODEL
    n_pages = PAST_LEN // PAGE_SIZE
    n_phys = (DEC_BATCH * n_pages * 5) // 4
    inp = {}
    inp['x_prompt'] = nrm((BATCH, SEQ, d), 1.0)
    inp['x_sample'] = nrm((DEC_BATCH, DEC_SEQ, d), 1.0)
    inp['cache_k'] = nrm((N_EVEN, n_phys, PAGE_SIZE, FOX_HEADS, FOX_HEAD_DIM), 1.0)
    inp['cache_v'] = nrm((N_EVEN, n_phys, PAGE_SIZE, FOX_HEADS, FOX_HEAD_DIM), 1.0)
    inp['cache_logf'] = jax.nn.log_sigmoid(nrm((N_EVEN, n_phys, PAGE_SIZE, FOX_HEADS), 1.0) + FOX_GATE_OFFSET)
    inp['state_hgrn'] = nrm((N_EVEN, DEC_BATCH, HG_HEADS, HG_DK, HG_DV), 0.3)
    inp['state_ssm_re'] = nrm((N_ODD, DEC_BATCH, S5_GROUPS, S5_STATE), 0.1)
    inp['state_ssm_im'] = nrm((N_ODD, DEC_BATCH, S5_GROUPS, S5_STATE), 0.1)
    perm = jax.random.permutation(next(kit), n_phys)[:DEC_BATCH * n_pages]
    inp['page_table'] = perm.reshape(DEC_BATCH, n_pages).astype(jnp.int32)
    inp['c_prompt'] = nrm((BATCH, d), 1.0)
    inp['c_sample'] = nrm((DEC_BATCH, d), 1.0)
    inp['norm1_w'] = 1.0 + nrm((DEPTH, d), 0.02)
    inp['norm2_w'] = 1.0 + nrm((DEPTH, d), 0.02)
    inp['w_ada'] = nrm((DEPTH, d, N_ADA * d), 0.5 * d ** -0.5)
    inp['b_ada'] = nrm((DEPTH, N_ADA * d), 0.02)
    inp['w_in_even'] = nrm((N_EVEN, d, IN_EVEN), d ** -0.5)
    inp['fox_fbias'] = FOX_GATE_OFFSET + nrm((N_EVEN, FOX_HEADS), 0.1)
    inp['q_norm_w'] = 1.0 + nrm((N_EVEN, FOX_HEAD_DIM), 0.02)
    inp['k_norm_w'] = 1.0 + nrm((N_EVEN, FOX_HEAD_DIM), 0.02)
    inp['hgrn_lb'] = nrm((N_EVEN, HG_W), 0.1)
    inp['hgrn_onorm_w'] = 1.0 + nrm((N_EVEN, HG_DV), 0.02)
    inp['w_out_even'] = nrm((N_EVEN, FOX_W + HG_W, d), (FOX_W + HG_W) ** -0.5)
    inp['s5_A_re'] = -0.5 + nrm((N_ODD, S5_GROUPS, S5_STATE), 0.01)
    inp['s5_A_im'] = math.pi * jnp.arange(S5_STATE, dtype=F32)[None, None, :] + nrm((N_ODD, S5_GROUPS, S5_STATE), 0.01)
    inp['s5_B_re'] = nrm((N_ODD, S5_GROUPS, S5_STATE, S5_GROUP), (2 * S5_GROUP) ** -0.5)
    inp['s5_B_im'] = nrm((N_ODD, S5_GROUPS, S5_STATE, S5_GROUP), (2 * S5_GROUP) ** -0.5)
    inp['s5_C_re'] = nrm((N_ODD, S5_GROUPS, S5_GROUP, S5_STATE), (2 * S5_STATE) ** -0.5)
    inp['s5_C_im'] = nrm((N_ODD, S5_GROUPS, S5_GROUP, S5_STATE), (2 * S5_STATE) ** -0.5)
    inp['s5_D'] = nrm((N_ODD, S5_GROUPS, S5_GROUP), 1.0)
    inp['s5_log_step'] = jax.random.uniform(next(kit), (N_ODD, S5_GROUPS), F32, math.log(1e-3), math.log(1e-1))
    inp['w_glu'] = nrm((N_ODD, d, 2 * d), d ** -0.5)
    inp['b_glu'] = nrm((N_ODD, 2 * d), 0.02)
    inp['peer_wq'] = nrm((DEPTH, d, PEER_HEADS * PEER_QDIM), d ** -0.5)
    inp['peer_subkeys'] = nrm((DEPTH, 2, PEER_NKEYS, PEER_QDIM // 2), (PEER_QDIM // 2) ** -0.5)
    inp['peer_u'] = nrm((DEPTH, PEER_EXPERTS, d), d ** -0.5)
    inp['peer_v'] = nrm((DEPTH, PEER_EXPERTS, d), PEER_HEADS ** -0.5)
    return inp


def reference(x_prompt, x_sample, cache_k, cache_v, cache_logf, state_hgrn, state_ssm_re, state_ssm_im,
              page_table, c_prompt, c_sample, norm1_w, norm2_w, w_ada, b_ada, w_in_even, fox_fbias,
              q_norm_w, k_norm_w, hgrn_lb, hgrn_onorm_w, w_out_even, s5_A_re, s5_A_im, s5_B_re, s5_B_im,
              s5_C_re, s5_C_im, s5_D, s5_log_step, w_glu, b_glu, peer_wq, peer_subkeys, peer_u, peer_v):
    W = dict(norm1_w=norm1_w, norm2_w=norm2_w, w_ada=w_ada, b_ada=b_ada, w_in_even=w_in_even,
             fox_fbias=fox_fbias, q_norm_w=q_norm_w, k_norm_w=k_norm_w, hgrn_lb=hgrn_lb,
             hgrn_onorm_w=hgrn_onorm_w, w_out_even=w_out_even, s5_A_re=s5_A_re, s5_A_im=s5_A_im,
             s5_B_re=s5_B_re, s5_B_im=s5_B_im, s5_C_re=s5_C_re, s5_C_im=s5_C_im, s5_D=s5_D,
             s5_log_step=s5_log_step, w_glu=w_glu, b_glu=b_glu, peer_wq=peer_wq,
             peer_subkeys=peer_subkeys, peer_u=peer_u, peer_v=peer_v)
    past = dict(cache_k=cache_k, cache_v=cache_v, cache_logf=cache_logf, page_table=page_table,
                state_hgrn=state_hgrn, state_ssm_re=state_ssm_re, state_ssm_im=state_ssm_im)
    y_prompt, k_p, v_p, lf_p, hg_p, sr_p, si_p = _trunk(x_prompt, c_prompt, None, W)
    y_sample, k_s, v_s, lf_s, hg_s, sr_s, si_s = _trunk(x_sample, c_sample, past, W)
    return (y_prompt, y_sample, k_p, v_p, lf_p, hg_p, sr_p, si_p, k_s, v_s, lf_s, hg_s, sr_s, si_s)
```

```python
import functools

import jax
import jax.numpy as jnp
from jax import lax
from jax.experimental import pallas as pl
from jax.experimental.pallas import tpu as pltpu

F32 = jnp.float32
BF16 = jnp.bfloat16
HIGHEST = lax.Precision.HIGHEST

EPS = 1e-6
MASK_VALUE = -1e30
LANES = 128
FOX_HEADS = 8
FOX_HEAD_DIM = 64
FOX_W = FOX_HEADS * FOX_HEAD_DIM
HG_HEADS = 4
HG_DK = 128
HG_W = HG_HEADS * HG_DK
S5_GROUP = 16
S5_STATE = 64
PEER_HEADS = 8
PEER_NKEYS = 128
PEER_TOPK = 16
PAGE_SIZE = 128
VMEM_LIMIT = 56 * 1024 * 1024

PROMPT_CHUNK = 16


def _cparams(*sem):
    return pltpu.CompilerParams(dimension_semantics=sem, vmem_limit_bytes=VMEM_LIMIT)


def _dot(a, b):
    return jnp.dot(a, b, preferred_element_type=F32)


def _dot_exact(a, b):
    return jnp.dot(a, b, precision=HIGHEST, preferred_element_type=F32)


def _dot_nt(a, b):
    return lax.dot_general(a, b, (((1,), (1,)), ((), ())), preferred_element_type=F32)


def _dot_tn(a, b):
    return lax.dot_general(a, b, (((0,), (0,)), ((), ())), preferred_element_type=F32)


def _sigmoid(x):
    return 1.0 / (1.0 + jnp.exp(-x))


def _gelu(x):
    return 0.5 * x * (1.0 + lax.erf(x * 0.7071067811865476))


def _modulate(x, w, shift, scale):
    ms = jnp.mean(x * x, axis=-1, keepdims=True)
    return (x * lax.rsqrt(ms + EPS) * w) * (1.0 + scale) + shift


def _iota(shape, dim):
    return lax.broadcasted_iota(jnp.int32, shape, dim)


def _mod_operand(vec, seq_len, tm, grid_rank=1):
    n_seq, d = vec.shape
    if seq_len >= tm:
        assert seq_len % tm == 0
        arr = vec.reshape(n_seq, 1, d)
        div = seq_len // tm
        rows = 1
    else:
        assert tm % seq_len == 0
        arr = jnp.repeat(vec, seq_len, axis=0).reshape((n_seq * seq_len) // tm, tm, d)
        div = 1
        rows = tm
    if grid_rank == 1:
        spec = pl.BlockSpec((None, rows, d), lambda i: (i // div, 0, 0))
    else:
        spec = pl.BlockSpec((None, rows, d), lambda i, c: (i // div, 0, 0))
    return arr, spec


def _ada_body(c_ref, w_ref, b_ref, o_ref):
    o_ref[...] = _dot(c_ref[...].astype(BF16), w_ref[...].astype(BF16)) + b_ref[...]


def _ada_op(c_all, w_ada, b_ada):
    depth, d, m = w_ada.shape
    r = c_all.shape[0]
    tn = 512
    return pl.pallas_call(
        _ada_body,
        grid=(depth, m // tn),
        in_specs=[pl.BlockSpec((r, d), lambda l, j: (0, 0)),
                  pl.BlockSpec((None, d, tn), lambda l, j: (l, 0, j)),
                  pl.BlockSpec((None, 1, tn), lambda l, j: (l, 0, j))],
        out_specs=pl.BlockSpec((None, r, tn), lambda l, j: (l, 0, j)),
        out_shape=jax.ShapeDtypeStruct((depth, r, m), F32),
        compiler_params=_cparams("parallel", "parallel"),
        name="adaln",
    )(c_all, w_ada, b_ada.reshape(depth, 1, m))


def _even_in_body(x_ref, sh_ref, sc_ref, nw_ref, w_ref, fb_ref, qnw_ref, knw_ref, lb_ref,
                  q_ref, k_ref, v_ref, lf_ref, cum_ref, hq_ref, hk_ref, hv_ref, bc_ref, sg_ref,
                  carry_ref, *, seq_len, chunk, layer_e):
    i = pl.program_id(0)
    tm = x_ref.shape[0]
    h = _modulate(x_ref[...], nw_ref[...], sh_ref[...], sc_ref[...]).astype(BF16)

    def proj(c0, width):
        return _dot(h, w_ref[:, c0:c0 + width])

    same_head = (_iota((FOX_W, FOX_W), 0) // FOX_HEAD_DIM) == (_iota((FOX_W, FOX_W), 1) // FOX_HEAD_DIM)
    ones_bd = same_head.astype(F32)

    def head_norm(f, w):
        ms = _dot_exact(f * f, ones_bd) * (1.0 / FOX_HEAD_DIM)
        return f * lax.rsqrt(ms + EPS) * w

    q_ref[...] = head_norm(proj(0, FOX_W), qnw_ref[...])
    k_ref[...] = head_norm(proj(FOX_W, FOX_W), knw_ref[...])
    v_ref[...] = proj(2 * FOX_W, FOX_W)

    z = proj(3 * FOX_W + 4 * HG_W, LANES) + fb_ref[...]
    logf = jnp.minimum(z, 0.0) - jnp.log1p(jnp.exp(-jnp.abs(z)))
    lf_ref[...] = logf
    rt = _iota((tm, tm), 0)
    ct = _iota((tm, tm), 1)
    if seq_len >= tm:
        tri = (ct <= rt).astype(F32)
    else:
        tri = jnp.where(ct <= rt, jnp.where((ct // seq_len) == (rt // seq_len), 1.0, 0.0), 0.0)
    cum = _dot_exact(tri, logf)
    if seq_len >= tm:
        seq_tiles = seq_len // tm

        @pl.when(i % seq_tiles == 0)
        def _():
            carry_ref[...] = jnp.zeros_like(carry_ref)

        cum = cum + carry_ref[...]
        carry_ref[...] = cum[tm - 1:tm, :]
    cum_ref[...] = cum

    base = 3 * FOX_W
    hq = proj(base, HG_W)
    hf = proj(base + HG_W, HG_W)
    hv_ref[...] = proj(base + 2 * HG_W, HG_W)
    hg = proj(base + 3 * HG_W, HG_W)
    lbp = lb_ref[...]
    lmax = jnp.max(lbp, axis=0, keepdims=True)
    lexp = jnp.exp(lbp - lmax)
    lsm = lexp / jnp.sum(lexp, axis=0, keepdims=True)
    lb = jnp.zeros((1, HG_W), F32)
    for r in range(1, layer_e + 1):
        lb = lb + lsm[r:r + 1, :]
    fgate = lb + (1.0 - lb) * _sigmoid(hf)
    logg = jnp.log(fgate)
    hk_ref[...] = 1.0 - fgate
    hq_ref[...] = hq * _sigmoid(hq)
    sg_ref[...] = hg * _sigmoid(hg)
    if chunk == seq_len and seq_len < tm:
        tri_c = tri
    else:
        tri_c = jnp.where(ct <= rt, jnp.where((ct // chunk) == (rt // chunk), 1.0, 0.0), 0.0)
    bc_ref[...] = _dot_exact(tri_c, logg)


def _even_in_op(x, sh, sc, nw, w_in, fb, qnw, knw, lb, *, seq_len, chunk, layer_e):
    n, d = x.shape
    tm = min(256, n)
    sh_a, mod_spec = _mod_operand(sh, seq_len, tm)
    sc_a, _ = _mod_operand(sc, seq_len, tm)
    wcols = w_in.shape[1]
    const = lambda shape: pl.BlockSpec(shape, lambda i: (0,) * len(shape))
    tok = lambda w: pl.BlockSpec((tm, w), lambda i: (i, 0))
    out_w = [FOX_W, FOX_W, FOX_W, LANES, LANES, HG_W, HG_W, HG_W, HG_W, HG_W]
    return pl.pallas_call(
        functools.partial(_even_in_body, seq_len=seq_len, chunk=chunk, layer_e=layer_e),
        grid=(n // tm,),
        in_specs=[tok(d), mod_spec, mod_spec, const((1, d)), const((d, wcols)), const((1, LANES)),
                  const((1, FOX_W)), const((1, FOX_W)), const(lb.shape)],
        out_specs=[tok(w) for w in out_w],
        out_shape=[jax.ShapeDtypeStruct((n, w), F32) for w in out_w],
        scratch_shapes=[pltpu.VMEM((1, LANES), F32)],
        compiler_params=_cparams("arbitrary"),
        name="even_in",
    )(x, sh_a, sc_a, nw, w_in, fb, qnw, knw, lb)


def _fox_prompt_body(q_ref, k_ref, v_ref, fq_ref, fk_ref, o_ref, m_sc, l_sc, acc_sc, *, scale):
    qi = pl.program_id(1)
    ki = pl.program_id(2)
    tq = q_ref.shape[0]
    tk = k_ref.shape[0]

    @pl.when(ki == 0)
    def _():
        m_sc[...] = jnp.full_like(m_sc, -jnp.inf)
        l_sc[...] = jnp.zeros_like(l_sc)
        acc_sc[...] = jnp.zeros_like(acc_sc)

    @pl.when(ki <= qi)
    def _():
        lane_head = _iota((tq, LANES), 1) // FOX_HEAD_DIM
        causal = (ki * tk + _iota((tq, tk), 1)) <= (qi * tq + _iota((tq, tk), 0))
        for h in range(FOX_HEADS):
            pair, hh = divmod(h, 2)
            cols = slice(pair * LANES, (pair + 1) * LANES)
            qh = jnp.where(lane_head == hh, q_ref[:, cols], 0.0).astype(BF16)
            s = _dot_nt(qh, k_ref[:, cols].astype(BF16)) * scale
            s = s + fq_ref[:, h:h + 1] - fk_ref[h:h + 1, :]
            s = jnp.where(causal, s, MASK_VALUE)
            m_old = m_sc[h]
            m_new = jnp.maximum(m_old, jnp.max(s, axis=-1, keepdims=True))
            alpha = jnp.exp(m_old - m_new)
            p = jnp.exp(s - m_new)
            l_sc[h] = alpha * l_sc[h] + jnp.sum(p, axis=-1, keepdims=True)
            m_sc[h] = m_new
            acc_sc[h] = alpha * acc_sc[h] + _dot(p.astype(BF16), v_ref[:, cols].astype(BF16))

    @pl.when(ki == qi)
    def _():
        lane_head = _iota((tq, LANES), 1) // FOX_HEAD_DIM
        for pair in range(FOX_HEADS // 2):
            o0 = acc_sc[2 * pair] / l_sc[2 * pair]
            o1 = acc_sc[2 * pair + 1] / l_sc[2 * pair + 1]
            o_ref[:, pair * LANES:(pair + 1) * LANES] = jnp.where(lane_head == 0, o0, o1)


def _fox_prompt_op(q, k, v, cum, cum_t, *, n_seq, seq_len):
    n = q.shape[0]
    t = min(512, seq_len)
    nt = seq_len // t
    qmap = lambda b, qi, ki: (b * nt + qi, 0)
    kmap = lambda b, qi, ki: (b * nt + jnp.minimum(ki, qi), 0)
    return pl.pallas_call(
        functools.partial(_fox_prompt_body, scale=FOX_HEAD_DIM ** -0.5),
        grid=(n_seq, nt, nt),
        in_specs=[pl.BlockSpec((t, FOX_W), qmap), pl.BlockSpec((t, FOX_W), kmap), pl.BlockSpec((t, FOX_W), kmap),
                  pl.BlockSpec((t, LANES), qmap),
                  pl.BlockSpec((None, FOX_HEADS, t), lambda b, qi, ki: (b, 0, jnp.minimum(ki, qi)))],
        out_specs=pl.BlockSpec((t, FOX_W), qmap),
        out_shape=jax.ShapeDtypeStruct((n, FOX_W), F32),
        scratch_shapes=[pltpu.VMEM((FOX_HEADS, t, 1), F32), pltpu.VMEM((FOX_HEADS, t, 1), F32),
                        pltpu.VMEM((FOX_HEADS, t, LANES), F32)],
        compiler_params=_cparams("parallel", "parallel", "arbitrary"),
        name="fox_prompt",
    )(q, k, v, cum, cum_t)


def _suffix_body(pt_ref, lp_ref, o_ref, carry_ref):
    jj = pl.program_id(1)

    @pl.when(jj == 0)
    def _():
        carry_ref[...] = jnp.zeros_like(carry_ref)

    lp = lp_ref[...]
    later = (_iota((PAGE_SIZE, PAGE_SIZE), 0) > _iota((PAGE_SIZE, PAGE_SIZE), 1)).astype(F32)
    o_ref[...] = _dot_exact(lp, later) + carry_ref[...]
    carry_ref[...] = carry_ref[...] + jnp.sum(lp, axis=-1, keepdims=True)


def _suffix_op(page_table, logf_t, layer_e):
    n_seq, n_pages = page_table.shape
    grid_spec = pltpu.PrefetchScalarGridSpec(
        num_scalar_prefetch=1,
        grid=(n_seq, n_pages),
        in_specs=[pl.BlockSpec((None, None, FOX_HEADS, PAGE_SIZE),
                               lambda b, jj, pt: (layer_e, pt[b, n_pages - 1 - jj], 0, 0))],
        out_specs=pl.BlockSpec((None, FOX_HEADS, PAGE_SIZE), lambda b, jj, pt: (b, 0, n_pages - 1 - jj)),
        scratch_shapes=[pltpu.VMEM((FOX_HEADS, PAGE_SIZE), F32)],
    )
    return pl.pallas_call(
        _suffix_body,
        grid_spec=grid_spec,
        out_shape=jax.ShapeDtypeStruct((n_seq, FOX_HEADS, n_pages * PAGE_SIZE), F32),
        compiler_params=_cparams("parallel", "arbitrary"),
        name="fox_suffix",
    )(page_table, logf_t)


def _fox_sample_body(pt_ref, q_ref, kn_ref, vn_ref, fq_ref, fkn_ref, suf_ref, *rest, n_pages, t_new, scale):
    k_refs = rest[:n_pages]
    v_refs = rest[n_pages:2 * n_pages]
    o_ref = rest[2 * n_pages]
    rows = FOX_HEADS * t_new
    past = n_pages * PAGE_SIZE
    row_head = _iota((rows, FOX_W), 0) // t_new
    col_head = _iota((rows, FOX_W), 1) // FOX_HEAD_DIM
    qbd = jnp.where(row_head == col_head, jnp.concatenate([q_ref[...]] * FOX_HEADS, axis=0), 0.0).astype(BF16)
    fq = fq_ref[:, 0:1]
    suf = suf_ref[...]
    bias = jnp.concatenate([jnp.broadcast_to(suf[h:h + 1, :], (t_new, past)) for h in range(FOX_HEADS)], axis=0)
    s_past = jnp.concatenate([_dot_nt(qbd, k_refs[j][...].astype(BF16)) for j in range(n_pages)], axis=1)
    s_past = s_past * scale + fq + bias
    pad = jnp.zeros((PAGE_SIZE - t_new, FOX_W), F32)
    kn = jnp.concatenate([kn_ref[...], pad], axis=0).astype(BF16)
    vn = jnp.concatenate([vn_ref[...], pad], axis=0).astype(BF16)
    s_new = _dot_nt(qbd, kn) * scale + fq - fkn_ref[...]
    key = _iota((rows, PAGE_SIZE), 1)
    tok = _iota((rows, PAGE_SIZE), 0) % t_new
    s_new = jnp.where(key <= tok, s_new, MASK_VALUE)
    m = jnp.maximum(jnp.max(s_past, axis=-1, keepdims=True), jnp.max(s_new, axis=-1, keepdims=True))
    p_past = jnp.exp(s_past - m)
    p_new = jnp.exp(s_new - m)
    denom = jnp.sum(p_past, axis=-1, keepdims=True) + jnp.sum(p_new, axis=-1, keepdims=True)
    o = _dot(p_new.astype(BF16), vn)
    for j in range(n_pages):
        o = o + _dot(p_past[:, j * PAGE_SIZE:(j + 1) * PAGE_SIZE].astype(BF16), v_refs[j][...].astype(BF16))
    o = o / denom
    out_head = _iota((t_new, FOX_W), 1) // FOX_HEAD_DIM
    out = jnp.zeros((t_new, FOX_W), F32)
    for h in range(FOX_HEADS):
        out = out + jnp.where(out_head == h, o[h * t_new:(h + 1) * t_new, :], 0.0)
    o_ref[...] = out


def _fox_sample_op(page_table, q, k_new, v_new, fq_rows, fk_new_rows, suffix, cache_k, cache_v, layer_e, t_new):
    n_seq, n_pages = page_table.shape
    rows = FOX_HEADS * t_new
    seq = lambda w: pl.BlockSpec((t_new, w), lambda b, pt: (b, 0))
    page_specs = [pl.BlockSpec((None, None, PAGE_SIZE, FOX_W), functools.partial(
        lambda b, pt, j: (layer_e, pt[b, j], 0, 0), j=j)) for j in range(n_pages)]
    grid_spec = pltpu.PrefetchScalarGridSpec(
        num_scalar_prefetch=1,
        grid=(n_seq,),
        in_specs=[seq(FOX_W), seq(FOX_W), seq(FOX_W),
                  pl.BlockSpec((None, rows, LANES), lambda b, pt: (b, 0, 0)),
                  pl.BlockSpec((None, rows, LANES), lambda b, pt: (b, 0, 0)),
                  pl.BlockSpec((None, FOX_HEADS, n_pages * PAGE_SIZE), lambda b, pt: (b, 0, 0))]
        + page_specs + page_specs,
        out_specs=seq(FOX_W),
    )
    return pl.pallas_call(
        functools.partial(_fox_sample_body, n_pages=n_pages, t_new=t_new, scale=FOX_HEAD_DIM ** -0.5),
        grid_spec=grid_spec,
        out_shape=jax.ShapeDtypeStruct(q.shape, F32),
        compiler_params=_cparams("parallel"),
        name="fox_sample",
    )(page_table, q, k_new, v_new, fq_rows, fk_new_rows, suffix, *([cache_k] * n_pages), *([cache_v] * n_pages))


def _hgrn_chunk(qc, kc, vc, bcc, state):
    n_rows = qc.shape[0]
    last = bcc[n_rows - 1:n_rows, :]
    o_inter = _dot((qc * jnp.exp(bcc)).astype(BF16), state.astype(BF16))
    kv = _dot_tn((kc * jnp.exp(last - bcc)).astype(BF16), vc.astype(BF16))
    decay_col = jnp.broadcast_to(jnp.exp(last), (HG_DK, HG_DK)).T
    return o_inter, decay_col * state + kv


def _hgrn_diag(q, k, v, bc, chunk):
    rows = q.shape[0]
    n = rows // chunk
    q3, k3, v3, b3 = (a.reshape(n, chunk, HG_DK) for a in (q, k, v, bc))
    sidx = _iota((n, chunk, HG_DK), 1)
    outs = []
    for t in range(chunk):
        decay = jnp.exp(jnp.where(sidx <= t, b3[:, t:t + 1, :] - b3, -jnp.inf))
        att = jnp.sum(decay * q3[:, t:t + 1, :] * k3, axis=-1, keepdims=True)
        outs.append(jnp.sum(att * v3, axis=1, keepdims=True))
    return jnp.concatenate(outs, axis=1).reshape(rows, HG_DK)


def _hgrn_finish(o, onw, sg):
    ms = jnp.mean(o * o, axis=-1, keepdims=True)
    return o * lax.rsqrt(ms + EPS) * onw * sg


def _hgrn_prompt_body(q_ref, k_ref, v_ref, bc_ref, sg_ref, onw_ref, o_ref, st_ref, s_sc, oi_sc, *, chunk):
    i = pl.program_id(0)
    n_seq, tm, _ = q_ref.shape

    @pl.when(i == 0)
    def _():
        s_sc[...] = jnp.zeros_like(s_sc)

    def chunk_body(c, carry):
        r0 = pl.multiple_of(c * chunk, chunk)
        for b in range(n_seq):
            for h in range(HG_HEADS):
                cols = slice(h * HG_DK, (h + 1) * HG_DK)
                o_inter, s_new = _hgrn_chunk(q_ref[b, pl.ds(r0, chunk), cols], k_ref[b, pl.ds(r0, chunk), cols],
                                             v_ref[b, pl.ds(r0, chunk), cols], bc_ref[b, pl.ds(r0, chunk), cols],
                                             s_sc[b * HG_HEADS + h])
                s_sc[b * HG_HEADS + h] = s_new
                oi_sc[b, pl.ds(r0, chunk), cols] = o_inter
        return carry

    lax.fori_loop(0, tm // chunk, chunk_body, 0)
    for b in range(n_seq):
        for h in range(HG_HEADS):
            cols = slice(h * HG_DK, (h + 1) * HG_DK)
            o = oi_sc[b, :, cols] + _hgrn_diag(q_ref[b, :, cols], k_ref[b, :, cols], v_ref[b, :, cols],
                                               bc_ref[b, :, cols], chunk)
            o_ref[b, :, cols] = _hgrn_finish(o, onw_ref[...], sg_ref[b, :, cols])
            st_ref[b, h] = s_sc[b * HG_HEADS + h]


def _hgrn_prompt_op(hq, hk, hv, bc, sg, onw, *, n_seq, seq_len, chunk):
    tm = min(512, seq_len)
    view = lambda a: a.reshape(n_seq, seq_len, HG_W)
    tok = pl.BlockSpec((n_seq, tm, HG_W), lambda i: (0, i, 0))
    o, st = pl.pallas_call(
        functools.partial(_hgrn_prompt_body, chunk=chunk),
        grid=(seq_len // tm,),
        in_specs=[tok] * 5 + [pl.BlockSpec((1, HG_DK), lambda i: (0, 0))],
        out_specs=[tok, pl.BlockSpec((n_seq, HG_HEADS, HG_DK, HG_DK), lambda i: (0, 0, 0, 0))],
        out_shape=[jax.ShapeDtypeStruct((n_seq, seq_len, HG_W), F32),
                   jax.ShapeDtypeStruct((n_seq, HG_HEADS, HG_DK, HG_DK), F32)],
        scratch_shapes=[pltpu.VMEM((n_seq * HG_HEADS, HG_DK, HG_DK), F32), pltpu.VMEM((n_seq, tm, HG_W), F32)],
        compiler_params=_cparams("arbitrary"),
        name="hgrn_prompt",
    )(view(hq), view(hk), view(hv), view(bc), view(sg), onw)
    return o.reshape(n_seq * seq_len, HG_W), st


def _hgrn_sample_body(q_ref, k_ref, v_ref, bc_ref, sg_ref, onw_ref, s0_ref, o_ref, st_ref, *, t_new):
    n_b = s0_ref.shape[0]
    for h in range(HG_HEADS):
        cols = slice(h * HG_DK, (h + 1) * HG_DK)
        parts = []
        for b in range(n_b):
            rows = slice(b * t_new, (b + 1) * t_new)
            o_inter, s_new = _hgrn_chunk(q_ref[rows, cols], k_ref[rows, cols], v_ref[rows, cols], bc_ref[rows, cols],
                                         s0_ref[b, h])
            st_ref[b, h] = s_new
            parts.append(o_inter)
        o = jnp.concatenate(parts, axis=0) + _hgrn_diag(q_ref[:, cols], k_ref[:, cols], v_ref[:, cols],
                                                        bc_ref[:, cols], t_new)
        o_ref[:, cols] = _hgrn_finish(o, onw_ref[...], sg_ref[:, cols])


def _hgrn_sample_op(hq, hk, hv, bc, sg, onw, s0, *, t_new):
    n_seq = s0.shape[0]
    n_b = min(8, n_seq)
    tok = pl.BlockSpec((n_b * t_new, HG_W), lambda i: (i, 0))
    st_spec = pl.BlockSpec((n_b, HG_HEADS, HG_DK, HG_DK), lambda i: (i, 0, 0, 0))
    return pl.pallas_call(
        functools.partial(_hgrn_sample_body, t_new=t_new),
        grid=(n_seq // n_b,),
        in_specs=[tok] * 5 + [pl.BlockSpec((1, HG_DK), lambda i: (0, 0)), st_spec],
        out_specs=[tok, st_spec],
        out_shape=[jax.ShapeDtypeStruct(hq.shape, F32), jax.ShapeDtypeStruct(s0.shape, F32)],
        compiler_params=_cparams("parallel"),
        name="hgrn_sample",
    )(hq, hk, hv, bc, sg, onw, s0)


def _residual_and_norm2(x, mix, g1, nw2, sh2, sc2, x1_ref, h2_ref):
    x1 = x + g1 * mix
    x1_ref[...] = x1
    h2_ref[...] = _modulate(x1, nw2, sh2, sc2).astype(BF16)


def _even_out_body(of_ref, oh_ref, x_ref, g1_ref, sh2_ref, sc2_ref, nw2_ref, wa_ref, wb_ref, x1_ref, h2_ref):
    mix = _dot(of_ref[...].astype(BF16), wa_ref[...]) + _dot(oh_ref[...].astype(BF16), wb_ref[...])
    _residual_and_norm2(x_ref[...], mix, g1_ref[...], nw2_ref[...], sh2_ref[...], sc2_ref[...], x1_ref, h2_ref)


def _odd_out_body(y_ref, x_ref, g1_ref, sh2_ref, sc2_ref, nw2_ref, w_ref, b_ref, x1_ref, h2_ref):
    d = x_ref.shape[1]
    ab = _dot(_gelu(y_ref[...]).astype(BF16), w_ref[...]) + b_ref[...]
    mix = ab[:, :d] * _sigmoid(ab[:, d:])
    _residual_and_norm2(x_ref[...], mix, g1_ref[...], nw2_ref[...], sh2_ref[...], sc2_ref[...], x1_ref, h2_ref)


def _mixer_out_op(body, name, token_inputs, x, g1, sh2, sc2, nw2, consts, *, seq_len):
    n, d = x.shape
    tm = min(512, n)
    mods = [_mod_operand(v, seq_len, tm) for v in (g1, sh2, sc2)]
    tok = lambda a: pl.BlockSpec((tm, a.shape[1]), lambda i: (i, 0))
    const = lambda a: pl.BlockSpec(a.shape, lambda i: (0,) * a.ndim)
    return pl.pallas_call(
        body,
        grid=(n // tm,),
        in_specs=[tok(a) for a in token_inputs] + [tok(x)] + [m[1] for m in mods] + [const(nw2)] + [const(a) for a in consts],
        out_specs=[tok(x), tok(x)],
        out_shape=[jax.ShapeDtypeStruct((n, d), F32), jax.ShapeDtypeStruct((n, d), BF16)],
        compiler_params=_cparams("parallel"),
        name=name,
    )(*token_inputs, x, *[m[0] for m in mods], nw2, *consts)


def _s5_prep_body(ar_ref, ai_ref, ls_ref, brt_ref, bit_ref, pwr_ref, pwi_ref, bbr_ref, bbi_ref):
    ar, ai = ar_ref[...], ai_ref[...]
    dt = jnp.exp(ls_ref[...])
    mag = jnp.exp(ar * dt)
    ang = ai * dt
    abr, abi = mag * jnp.cos(ang), mag * jnp.sin(ang)
    nr, ni = abr - 1.0, abi
    den = ar * ar + ai * ai
    cr = (nr * ar + ni * ai) / den
    ci = (ni * ar - nr * ai) / den
    bbr_ref[...] = cr * brt_ref[...] - ci * bit_ref[...]
    bbi_ref[...] = cr * bit_ref[...] + ci * brt_ref[...]
    pr, pi = abr, abi
    for k in range(pwr_ref.shape[0]):
        pwr_ref[k:k + 1, :] = pr
        pwi_ref[k:k + 1, :] = pi
        pr, pi = pr * abr - pi * abi, pr * abi + pi * abr


def _s5_prep_op(a_re, a_im, log_step, b_re, b_im, n_pow):
    groups, n_state = a_re.shape
    w = groups * n_state
    flat = lambda a: a.reshape(1, w)
    ls = flat(jnp.broadcast_to(log_step[:, None], (groups, n_state)))
    bt = lambda b: jnp.transpose(b, (2, 0, 1)).reshape(S5_GROUP, w)
    return pl.pallas_call(
        _s5_prep_body,
        out_shape=[jax.ShapeDtypeStruct((n_pow, w), F32), jax.ShapeDtypeStruct((n_pow, w), F32),
                   jax.ShapeDtypeStruct((S5_GROUP, w), F32), jax.ShapeDtypeStruct((S5_GROUP, w), F32)],
        name="s5_prep",
    )(flat(a_re), flat(a_im), ls, bt(b_re), bt(b_im))


def _s5_body(x_ref, sh_ref, sc_ref, nw_ref, bre_ref, bim_ref, cre_ref, cim_ref, d_ref, pwr_ref, pwi_ref,
             x0r_ref, x0i_ref, y_ref, sr_ref, si_ref, xr_sc, xi_sc, str_sc, sti_sc, car_r, car_i, *, carry):
    jt = pl.program_id(1)
    n_l, n_j, d = x_ref.shape
    rows = n_l * n_j
    n_tiles = d // LANES
    sw = xr_sc.shape[1] // n_tiles
    u = _modulate(x_ref[...], nw_ref[...], sh_ref[...], sc_ref[...]).reshape(rows, d)
    ub = u.astype(BF16)
    for r in range(n_tiles):
        xr_sc[:, r * sw:(r + 1) * sw] = _dot(ub[:, r * LANES:(r + 1) * LANES], bre_ref[r])
        xi_sc[:, r * sw:(r + 1) * sw] = _dot(ub[:, r * LANES:(r + 1) * LANES], bim_ref[r])
    ar, ai = pwr_ref[0:1, :], pwi_ref[0:1, :]
    for tl in range(1, n_l):
        prev = slice((tl - 1) * n_j, tl * n_j)
        cur = slice(tl * n_j, (tl + 1) * n_j)
        pr, pi = xr_sc[prev, :], xi_sc[prev, :]
        xr_sc[cur, :] = ar * pr - ai * pi + xr_sc[cur, :]
        xi_sc[cur, :] = ar * pi + ai * pr + xi_sc[cur, :]
    if carry:
        @pl.when(jt == 0)
        def _():
            car_r[...] = jnp.zeros_like(car_r)
            car_i[...] = jnp.zeros_like(car_i)

        plr, pli = pwr_ref[n_l - 1:n_l, :], pwi_ref[n_l - 1:n_l, :]
        last0 = (n_l - 1) * n_j

        def chunk_body(j, c):
            cr, ci = car_r[...], car_i[...]
            str_sc[pl.ds(j, 1), :] = cr
            sti_sc[pl.ds(j, 1), :] = ci
            car_r[...] = plr * cr - pli * ci + xr_sc[pl.ds(last0 + j, 1), :]
            car_i[...] = plr * ci + pli * cr + xi_sc[pl.ds(last0 + j, 1), :]
            return c

        lax.fori_loop(0, n_j, chunk_body, 0)
        sr_ref[...] = car_r[...]
        si_ref[...] = car_i[...]
    else:
        str_sc[...] = x0r_ref[...]
        sti_sc[...] = x0i_ref[...]
    s0r, s0i = str_sc[...], sti_sc[...]
    for tl in range(n_l):
        cur = slice(tl * n_j, (tl + 1) * n_j)
        pr, pi = pwr_ref[tl:tl + 1, :], pwi_ref[tl:tl + 1, :]
        xr_sc[cur, :] = xr_sc[cur, :] + (pr * s0r - pi * s0i)
        xi_sc[cur, :] = xi_sc[cur, :] + (pr * s0i + pi * s0r)
    if not carry:
        last = slice((n_l - 1) * n_j, n_l * n_j)
        sr_ref[...] = xr_sc[last, :]
        si_ref[...] = xi_sc[last, :]
    for r in range(n_tiles):
        cols = slice(r * LANES, (r + 1) * LANES)
        y = _dot(xr_sc[:, r * sw:(r + 1) * sw].astype(BF16), cre_ref[r]) \
            - _dot(xi_sc[:, r * sw:(r + 1) * sw].astype(BF16), cim_ref[r])
        y = y + d_ref[:, cols] * u[:, cols]
        y_ref[:, :, cols] = y.reshape(n_l, n_j, LANES)


def _s5_op(x, sh, sc, nw, mats, pw_r, pw_i, x0, *, n_seq, seq_len, chunk):
    n, d = x.shape
    bre, bim, cre, cim, dsk = mats
    w = pw_r.shape[1]
    carry = x0 is None
    if carry:
        n_b, n_j_all = n_seq, seq_len // chunk
    else:
        assert seq_len == chunk
        n_b, n_j_all = 1, n_seq
    xp = jnp.transpose(x.reshape(n_b, n_j_all, chunk, d), (0, 2, 1, 3))
    n_j = min(512 // chunk, n_j_all)
    if carry:
        mod = lambda v: (v.reshape(n_seq, 1, d), pl.BlockSpec((None, 1, d), lambda b, j: (b, 0, 0)))
        x0r = x0i = jnp.zeros((8, LANES), F32)
        x0_spec = pl.BlockSpec((8, LANES), lambda b, j: (0, 0))
        st_shape = jax.ShapeDtypeStruct((n_b, 1, w), F32)
        st_spec = pl.BlockSpec((None, 1, w), lambda b, j: (b, 0, 0))
    else:
        mod = lambda v: (v.reshape(n_j_all // n_j, n_j, d), pl.BlockSpec((None, n_j, d), lambda b, j: (j, 0, 0)))
        x0r, x0i = x0
        x0_spec = pl.BlockSpec((n_j, w), lambda b, j: (j, 0))
        st_shape = jax.ShapeDtypeStruct((n_seq, w), F32)
        st_spec = pl.BlockSpec((n_j, w), lambda b, j: (j, 0))
    (sh_a, mod_spec), (sc_a, _) = mod(sh), mod(sc)
    const = lambda a: pl.BlockSpec(a.shape, lambda b, j: (0,) * a.ndim)
    xspec = pl.BlockSpec((None, chunk, n_j, d), lambda b, j: (b, 0, j, 0))
    rows = chunk * n_j
    y, sr, si = pl.pallas_call(
        functools.partial(_s5_body, carry=carry),
        grid=(n_b, n_j_all // n_j),
        in_specs=[xspec, mod_spec, mod_spec, const(nw), const(bre), const(bim), const(cre), const(cim), const(dsk),
                  const(pw_r), const(pw_i), x0_spec, x0_spec],
        out_specs=[xspec, st_spec, st_spec],
        out_shape=[jax.ShapeDtypeStruct(xp.shape, F32), st_shape, st_shape],
        scratch_shapes=[pltpu.VMEM((rows, w), F32), pltpu.VMEM((rows, w), F32),
                        pltpu.VMEM((n_j, w), F32), pltpu.VMEM((n_j, w), F32),
                        pltpu.VMEM((1, w), F32), pltpu.VMEM((1, w), F32)],
        compiler_params=_cparams("parallel", "arbitrary"),
        name="s5_scan",
    )(xp, sh_a, sc_a, nw, bre, bim, cre, cim, dsk, pw_r, pw_i, x0r, x0i)
    y = jnp.transpose(y, (0, 2, 1, 3)).reshape(n, d)
    return y, sr.reshape(-1, w), si.reshape(-1, w)


def _s5_matrices(bbr_t, bbi_t, c_re, c_im, d_skip):
    groups = c_re.shape[0]
    per_tile = LANES // S5_GROUP
    n_tiles = groups // per_tile
    eye = jnp.eye(per_tile, dtype=F32)

    def b_mat(bt):
        b4 = jnp.transpose(bt.reshape(S5_GROUP, n_tiles, per_tile, S5_STATE), (1, 2, 0, 3))
        m = b4[:, :, :, None, :] * eye[None, :, None, :, None]
        return m.reshape(n_tiles, LANES, per_tile * S5_STATE).astype(BF16)

    def c_mat(c):
        c4 = jnp.transpose(c.reshape(n_tiles, per_tile, S5_GROUP, S5_STATE), (0, 3, 1, 2))
        m = c4[:, None, :, :, :] * eye[None, :, None, :, None]
        return m.reshape(n_tiles, per_tile * S5_STATE, LANES).astype(BF16)

    return b_mat(bbr_t), b_mat(bbi_t), c_mat(c_re), c_mat(c_im), d_skip.reshape(1, groups * S5_GROUP)


def _top_rows(s, count):
    out = []
    for _ in range(count):
        m = jnp.max(s, axis=0, keepdims=True)
        out.append(m)
        s = jnp.where(s == m, -jnp.inf, s)
    return jnp.concatenate(out, axis=0)


def _peer_body(h2_ref, x1_ref, g2_ref, wq_ref, keys_ref, u_ref, vt_ref, o_ref,
               s2_sc, thr_sc, e1_sc, e2_sc, w_sc, acc_sc):
    c = pl.program_id(1)
    n_chunks = pl.num_programs(1)
    tm = h2_ref.shape[0]
    e_c = u_ref.shape[0]
    per_chunk = e_c // PEER_NKEYS
    h2 = h2_ref[...]

    @pl.when(c == 0)
    def _():
        acc_sc[...] = jnp.zeros_like(acc_sc)
        q = _dot(h2, wq_ref[...]).astype(BF16)
        for h in range(PEER_HEADS):
            s1 = _dot_nt(keys_ref[0], q[:, (2 * h) * LANES:(2 * h + 1) * LANES])
            s2 = _dot_nt(keys_ref[1], q[:, (2 * h + 1) * LANES:(2 * h + 2) * LANES])
            v1 = _top_rows(s1, PEER_TOPK)
            v2 = _top_rows(s2, PEER_TOPK)
            cand = jnp.concatenate([v1[a:a + 1, :] + v2 for a in range(PEER_TOPK)], axis=0)
            best = _top_rows(cand, PEER_TOPK + 1)
            thr = 0.5 * (best[PEER_TOPK - 1:PEER_TOPK, :] + best[PEER_TOPK:PEER_TOPK + 1, :])
            z = jnp.sum(jnp.where(cand > thr, jnp.exp(cand - best[0:1, :]), 0.0), axis=0, keepdims=True)
            s2_sc[h] = s2
            thr_sc[h] = thr - s1
            e1_sc[h] = jnp.exp(s1 - v1[0:1, :]) / z
            e2_sc[h] = jnp.exp(s2 - v2[0:1, :])

    act = _dot_nt(u_ref[...], h2)
    for il in range(per_chunk):
        i = c * per_chunk + il
        gate = jnp.zeros((PEER_NKEYS, tm), F32)
        for h in range(PEER_HEADS):
            sel = s2_sc[h] > thr_sc[h, pl.ds(i, 1), :]
            gate = gate + jnp.where(sel, e2_sc[h], 0.0) * e1_sc[h, pl.ds(i, 1), :]
        rows = slice(il * PEER_NKEYS, (il + 1) * PEER_NKEYS)
        w_sc[rows, :] = (gate * _gelu(act[rows, :])).astype(BF16)
    acc_sc[...] += _dot(vt_ref[...], w_sc[...])

    @pl.when(c == n_chunks - 1)
    def _():
        o_ref[...] = x1_ref[...] + g2_ref[...] * acc_sc[...].T


def _peer_op(h2, x1, g2, wq, keys, u, vt, *, seq_len):
    n, d = x1.shape
    tm = min(512, n)
    e_c = 1024
    n_exp = u.shape[0]
    g2_a, g2_spec = _mod_operand(g2, seq_len, tm, grid_rank=2)
    const = lambda a: pl.BlockSpec(a.shape, lambda i, c: (0,) * a.ndim)
    tok = lambda a: pl.BlockSpec((tm, a.shape[1]), lambda i, c: (i, 0))
    head_scr = pltpu.VMEM((PEER_HEADS, PEER_NKEYS, tm), F32)
    return pl.pallas_call(
        _peer_body,
        grid=(n // tm, n_exp // e_c),
        in_specs=[tok(h2), tok(x1), g2_spec, const(wq), const(keys),
                  pl.BlockSpec((e_c, d), lambda i, c: (c, 0)), pl.BlockSpec((d, e_c), lambda i, c: (0, c))],
        out_specs=tok(x1),
        out_shape=jax.ShapeDtypeStruct((n, d), F32),
        scratch_shapes=[head_scr, head_scr, head_scr, head_scr,
                        pltpu.VMEM((e_c, tm), BF16), pltpu.VMEM((d, tm), F32)],
        compiler_params=_cparams("parallel", "arbitrary"),
        name="peer",
    )(h2, x1, g2_a, wq, keys, u, vt)


def _trunk(x, mods, past, wts, *, n_seq, seq_len, chunk):
    depth = len(mods)
    ks, vs, lfs, hs, srs, sis = [], [], [], [], [], []
    for l in range(depth):
        sh1, sc1, g1, sh2, sc2, g2 = mods[l]
        nw1, nw2 = wts["norm1_w"][l:l + 1], wts["norm2_w"][l:l + 1]
        if l % 2 == 0:
            e = l // 2
            q, k, v, lf, cum, hq, hk, hv, bc, sg = _even_in_op(
                x, sh1, sc1, nw1, wts["w_in"][e], wts["fbias"][e], wts["qnw"][e], wts["knw"][e], wts["hgrn_lb"],
                seq_len=seq_len, chunk=chunk, layer_e=e)
            cum8 = cum[:, :FOX_HEADS].reshape(n_seq, seq_len, FOX_HEADS)
            if past is None:
                o_fox = _fox_prompt_op(q, k, v, cum, jnp.transpose(cum8, (0, 2, 1)), n_seq=n_seq, seq_len=seq_len)
                o_h, s_new = _hgrn_prompt_op(hq, hk, hv, bc, sg, wts["onw"][e], n_seq=n_seq, seq_len=seq_len, chunk=chunk)
            else:
                per_head = jnp.transpose(cum8, (0, 2, 1))
                rows = FOX_HEADS * seq_len
                fq_rows = jnp.broadcast_to(per_head.reshape(n_seq, rows, 1), (n_seq, rows, LANES))
                fk_new = jnp.pad(jnp.repeat(per_head, seq_len, axis=1), ((0, 0), (0, 0), (0, LANES - seq_len)))
                suffix = _suffix_op(past["page_table"], past["logf_t"], e)
                o_fox = _fox_sample_op(past["page_table"], q, k, v, fq_rows, fk_new, suffix,
                                       past["cache_k"], past["cache_v"], e, seq_len)
                o_h, s_new = _hgrn_sample_op(hq, hk, hv, bc, sg, wts["onw"][e], past["state_hgrn"][e], t_new=seq_len)
            ks.append(k)
            vs.append(v)
            lfs.append(lf[:, :FOX_HEADS])
            hs.append(s_new)
            x1, h2 = _mixer_out_op(_even_out_body, "even_out", [o_fox, o_h], x, g1, sh2, sc2, nw2,
                                   [wts["w_out_a"][e], wts["w_out_b"][e]], seq_len=seq_len)
        else:
            o = l // 2
            x0 = None if past is None else (past["state_ssm_re"][o], past["state_ssm_im"][o])
            y, sr, si = _s5_op(x, sh1, sc1, nw1, wts["s5_mats"][o], wts["s5_pw_r"][o], wts["s5_pw_i"][o], x0,
                               n_seq=n_seq, seq_len=seq_len, chunk=chunk)
            srs.append(sr)
            sis.append(si)
            x1, h2 = _mixer_out_op(_odd_out_body, "odd_out", [y], x, g1, sh2, sc2, nw2,
                                   [wts["w_glu"][o], wts["b_glu"][o]], seq_len=seq_len)
        x = _peer_op(h2, x1, g2, wts["peer_wq"][l], wts["peer_keys"][l], wts["peer_u"][l], wts["peer_vt"][l],
                     seq_len=seq_len)
    return x, ks, vs, lfs, hs, srs, sis


def kernel(x_prompt, x_sample, cache_k, cache_v, cache_logf, state_hgrn, state_ssm_re, state_ssm_im, page_table,
           c_prompt, c_sample, norm1_w, norm2_w, w_ada, b_ada, w_in_even, fox_fbias, q_norm_w, k_norm_w, hgrn_lb,
           hgrn_onorm_w, w_out_even, s5_A_re, s5_A_im, s5_B_re, s5_B_im, s5_C_re, s5_C_im, s5_D, s5_log_step,
           w_glu, b_glu, peer_wq, peer_subkeys, peer_u, peer_v):
    n_p_seq, t_p, d = x_prompt.shape
    n_s_seq, t_s, _ = x_sample.shape
    depth = w_ada.shape[0]
    n_even = w_in_even.shape[0]
    n_odd = w_glu.shape[0]
    n_ada = w_ada.shape[2] // d

    sizes = (FOX_W, FOX_W, FOX_W, FOX_HEADS, HG_W, HG_W, HG_W, HG_W)
    offs = [0]
    for s in sizes:
        offs.append(offs[-1] + s)
    seg = lambda i: w_in_even[:, :, offs[i]:offs[i + 1]]
    w_in = jnp.concatenate([seg(0), seg(1), seg(2), seg(4), seg(5), seg(6), seg(7),
                            jnp.pad(seg(3), ((0, 0), (0, 0), (0, LANES - FOX_HEADS)))], axis=-1).astype(BF16)
    wts = dict(
        norm1_w=norm1_w, norm2_w=norm2_w, w_in=w_in,
        fbias=jnp.pad(fox_fbias, ((0, 0), (0, LANES - FOX_HEADS))).reshape(n_even, 1, LANES),
        qnw=jnp.tile(q_norm_w, (1, FOX_HEADS)).reshape(n_even, 1, FOX_W),
        knw=jnp.tile(k_norm_w, (1, FOX_HEADS)).reshape(n_even, 1, FOX_W),
        hgrn_lb=hgrn_lb, onw=hgrn_onorm_w.reshape(n_even, 1, HG_DK),
        w_out_a=w_out_even[:, :FOX_W].astype(BF16), w_out_b=w_out_even[:, FOX_W:].astype(BF16),
        w_glu=w_glu.astype(BF16), b_glu=b_glu.reshape(n_odd, 1, 2 * d),
        peer_wq=peer_wq.astype(BF16), peer_keys=peer_subkeys.astype(BF16),
        peer_u=peer_u.astype(BF16), peer_vt=jnp.transpose(peer_v.astype(BF16), (0, 2, 1)),
    )
    mats, pws_r, pws_i = [], [], []
    for o in range(n_odd):
        pw_r, pw_i, bbr_t, bbi_t = _s5_prep_op(s5_A_re[o], s5_A_im[o], s5_log_step[o], s5_B_re[o], s5_B_im[o],
                                               max(PROMPT_CHUNK, t_s))
        mats.append(_s5_matrices(bbr_t, bbi_t, s5_C_re[o], s5_C_im[o], s5_D[o]))
        pws_r.append(pw_r)
        pws_i.append(pw_i)
    wts.update(s5_mats=mats, s5_pw_r=pws_r, s5_pw_i=pws_i)

    n_c = n_p_seq + n_s_seq
    c_all = jnp.pad(jnp.concatenate([c_prompt, c_sample], axis=0), ((0, (-n_c) % 8), (0, 0)))
    mod = _ada_op(c_all, w_ada, b_ada)
    mods_p = [[mod[l, :n_p_seq, a * d:(a + 1) * d] for a in range(n_ada)] for l in range(depth)]
    mods_s = [[mod[l, n_p_seq:n_c, a * d:(a + 1) * d] for a in range(n_ada)] for l in range(depth)]

    n_phys = cache_k.shape[1]
    past = dict(page_table=page_table,
                cache_k=cache_k.reshape(n_even, n_phys, PAGE_SIZE, FOX_W),
                cache_v=cache_v.reshape(n_even, n_phys, PAGE_SIZE, FOX_W),
                logf_t=jnp.transpose(cache_logf, (0, 1, 3, 2)),
                state_hgrn=state_hgrn,
                state_ssm_re=state_ssm_re.reshape(n_odd, n_s_seq, -1),
                state_ssm_im=state_ssm_im.reshape(n_odd, n_s_seq, -1))

    yp, kp, vp, lfp, hp, srp, sip = _trunk(x_prompt.reshape(n_p_seq * t_p, d), mods_p, None, wts,
                                           n_seq=n_p_seq, seq_len=t_p, chunk=PROMPT_CHUNK)
    ys, ksm, vsm, lfsm, hsm, srs, sis = _trunk(x_sample.reshape(n_s_seq * t_s, d), mods_s, past, wts,
                                               n_seq=n_s_seq, seq_len=t_s, chunk=t_s)

    def pack(n_seq, t, ks, vs, lfs, hs, srs_, sis_):
        return (jnp.stack(ks).reshape(n_even, n_seq, t, FOX_HEADS, FOX_HEAD_DIM),
                jnp.stack(vs).reshape(n_even, n_seq, t, FOX_HEADS, FOX_HEAD_DIM),
                jnp.stack(lfs).reshape(n_even, n_seq, t, FOX_HEADS),
                jnp.stack(hs),
                jnp.stack(srs_).reshape(n_odd, n_seq, -1, S5_STATE),
                jnp.stack(sis_).reshape(n_odd, n_seq, -1, S5_STATE))

    return ((yp.reshape(n_p_seq, t_p, d), ys.reshape(n_s_seq, t_s, d))
            + pack(n_p_seq, t_p, kp, vp, lfp, hp, srp, sip)
            + pack(n_s_seq, t_s, ksm, vsm, lfsm, hsm, srs, sis))
```

```python
import functools

import jax
import jax.numpy as jnp
import numpy as np
from jax import lax
from jax.experimental import pallas as pl
from jax.experimental.pallas import tpu as pltpu

F32 = jnp.float32
BF16 = jnp.bfloat16
HIGHEST = lax.Precision.HIGHEST

EPS = 1e-6
MASK_VALUE = -1e30
LANES = 128
FOX_HEADS = 8
FOX_HEAD_DIM = 64
FOX_W = FOX_HEADS * FOX_HEAD_DIM
HG_HEADS = 4
HG_DK = 128
HG_W = HG_HEADS * HG_DK
S5_GROUP = 16
S5_STATE = 64
PEER_HEADS = 8
PEER_NKEYS = 128
PEER_TOPK = 16
PAGE_SIZE = 128
VMEM_LIMIT = 56 * 1024 * 1024

PROMPT_CHUNK = 16
PEER_EXPERT_CHUNK = 1024
PEER_TOKEN_SPLIT = 256
RANK_NONE = 255.0


def _cparams(*sem):
    return pltpu.CompilerParams(dimension_semantics=sem, vmem_limit_bytes=VMEM_LIMIT)


def _dot(a, b):
    return jnp.dot(a, b, preferred_element_type=F32)


def _dot_exact(a, b):
    return jnp.dot(a, b, precision=HIGHEST, preferred_element_type=F32)


def _dot_nt(a, b):
    return lax.dot_general(a, b, (((1,), (1,)), ((), ())), preferred_element_type=F32)


def _dot_tn(a, b):
    return lax.dot_general(a, b, (((0,), (0,)), ((), ())), preferred_element_type=F32)


def _sigmoid(x):
    return 1.0 / (1.0 + jnp.exp(-x))


def _gelu(x):
    return 0.5 * x * (1.0 + lax.erf(x * 0.7071067811865476))


def _modulate(x, w, shift, scale):
    ms = jnp.mean(x * x, axis=-1, keepdims=True)
    return (x * lax.rsqrt(ms + EPS) * w) * (1.0 + scale) + shift


def _iota(shape, dim):
    return lax.broadcasted_iota(jnp.int32, shape, dim)


def _mod_operand(vec, seq_len, tm, grid_rank=1):
    n_seq, d = vec.shape
    if seq_len >= tm:
        assert seq_len % tm == 0
        arr = vec.reshape(n_seq, 1, d)
        div = seq_len // tm
        rows = 1
    else:
        assert tm % seq_len == 0
        arr = jnp.repeat(vec, seq_len, axis=0).reshape((n_seq * seq_len) // tm, tm, d)
        div = 1
        rows = tm
    if grid_rank == 1:
        spec = pl.BlockSpec((None, rows, d), lambda i: (i // div, 0, 0))
    else:
        spec = pl.BlockSpec((None, rows, d), lambda i, c: (i // div, 0, 0))
    return arr, spec


def _ada_body(c_ref, w_ref, b_ref, o_ref):
    o_ref[...] = _dot(c_ref[...].astype(BF16), w_ref[...].astype(BF16)) + b_ref[...]


def _ada_op(c_all, w_ada, b_ada):
    depth, d, m = w_ada.shape
    r = c_all.shape[0]
    tn = 512
    return pl.pallas_call(
        _ada_body,
        grid=(depth, m // tn),
        in_specs=[pl.BlockSpec((r, d), lambda l, j: (0, 0)),
                  pl.BlockSpec((None, d, tn), lambda l, j: (l, 0, j)),
                  pl.BlockSpec((None, 1, tn), lambda l, j: (l, 0, j))],
        out_specs=pl.BlockSpec((None, r, tn), lambda l, j: (l, 0, j)),
        out_shape=jax.ShapeDtypeStruct((depth, r, m), F32),
        compiler_params=_cparams("parallel", "parallel"),
        name="adaln",
    )(c_all, w_ada, b_ada.reshape(depth, 1, m))


def _fox_operand_layout():
    place = np.zeros((FOX_W, FOX_HEADS * LANES), np.float32)
    bias_q = np.zeros((3 * LANES, FOX_HEADS * LANES), np.float32)
    bias_k = np.zeros((3 * LANES, FOX_HEADS * LANES), np.float32)
    ones_q = np.zeros((1, FOX_HEADS * LANES), np.float32)
    ones_k = np.zeros((1, FOX_HEADS * LANES), np.float32)
    for h in range(FOX_HEADS):
        for dd in range(FOX_HEAD_DIM):
            place[h * FOX_HEAD_DIM + dd, h * LANES + dd] = 1.0
        for part in range(3):
            bias_q[part * LANES + h, h * LANES + FOX_HEAD_DIM + part] = 1.0
            bias_k[part * LANES + h, h * LANES + FOX_HEAD_DIM + 3 + part] = -1.0
            ones_k[0, h * LANES + FOX_HEAD_DIM + part] = 1.0
            ones_q[0, h * LANES + FOX_HEAD_DIM + 3 + part] = 1.0
    return (jnp.asarray(place, BF16), jnp.asarray(bias_q, BF16), jnp.asarray(bias_k, BF16),
            jnp.asarray(ones_q), jnp.asarray(ones_k))


def _even_in_body(x_ref, sh_ref, sc_ref, nw_ref, w_ref, fb_ref, qnw_ref, knw_ref, lb_ref, *rest,
                  seq_len, chunk, layer_e, attn_operands):
    if attn_operands:
        place_ref, bq_ref, bk_ref, oq_ref, ok_ref = rest[:5]
        rest = rest[5:]
    q_ref, k_ref, v_ref, lf_ref, cum_ref, hq_ref, hk_ref, hv_ref, bc_ref, sg_ref = rest[:10]
    carry_ref = rest[-1]
    i = pl.program_id(0)
    tm = x_ref.shape[0]
    h = _modulate(x_ref[...], nw_ref[...], sh_ref[...], sc_ref[...]).astype(BF16)

    def proj(c0, width):
        return _dot(h, w_ref[:, c0:c0 + width])

    same_head = (_iota((FOX_W, FOX_W), 0) // FOX_HEAD_DIM) == (_iota((FOX_W, FOX_W), 1) // FOX_HEAD_DIM)
    ones_bd = same_head.astype(F32)

    def head_norm(f, w):
        ms = _dot_exact(f * f, ones_bd) * (1.0 / FOX_HEAD_DIM)
        return f * lax.rsqrt(ms + EPS) * w

    qn = head_norm(proj(0, FOX_W), qnw_ref[...])
    kn = head_norm(proj(FOX_W, FOX_W), knw_ref[...])
    v = proj(2 * FOX_W, FOX_W)
    q_ref[...] = qn
    k_ref[...] = kn
    v_ref[...] = v

    z = proj(3 * FOX_W + 4 * HG_W, LANES) + fb_ref[...]
    logf = jnp.minimum(z, 0.0) - jnp.log1p(jnp.exp(-jnp.abs(z)))
    lf_ref[...] = logf
    rt = _iota((tm, tm), 0)
    ct = _iota((tm, tm), 1)
    if seq_len >= tm:
        tri = (ct <= rt).astype(F32)
    else:
        tri = jnp.where(ct <= rt, jnp.where((ct // seq_len) == (rt // seq_len), 1.0, 0.0), 0.0)
    cum = _dot_exact(tri, logf)
    if seq_len >= tm:
        seq_tiles = seq_len // tm

        @pl.when(i % seq_tiles == 0)
        def _():
            carry_ref[...] = jnp.zeros_like(carry_ref)

        cum = cum + carry_ref[...]
        carry_ref[...] = cum[tm - 1:tm, :]
    cum_ref[...] = cum
    if attn_operands:
        qa_ref, ka_ref, vb_ref = rest[10:13]
        hi = cum.astype(BF16)
        rem = cum - hi.astype(F32)
        mid = rem.astype(BF16)
        lo = (rem - mid.astype(F32)).astype(BF16)
        cum3 = jnp.concatenate([hi, mid, lo], axis=1)
        scale = FOX_HEAD_DIM ** -0.5
        qa = _dot((qn * scale).astype(BF16), place_ref[...]) + _dot(cum3, bq_ref[...]) + oq_ref[...]
        ka = _dot(kn.astype(BF16), place_ref[...]) + _dot(cum3, bk_ref[...]) + ok_ref[...]
        qa_ref[...] = qa.astype(BF16)
        ka_ref[...] = ka.astype(BF16)
        vb_ref[...] = v.astype(BF16)

    base = 3 * FOX_W
    hq = proj(base, HG_W)
    hf = proj(base + HG_W, HG_W)
    hv_ref[...] = proj(base + 2 * HG_W, HG_W)
    hg = proj(base + 3 * HG_W, HG_W)
    lbp = lb_ref[...]
    lmax = jnp.max(lbp, axis=0, keepdims=True)
    lexp = jnp.exp(lbp - lmax)
    lsm = lexp / jnp.sum(lexp, axis=0, keepdims=True)
    lb = jnp.zeros((1, HG_W), F32)
    for r in range(1, layer_e + 1):
        lb = lb + lsm[r:r + 1, :]
    fgate = lb + (1.0 - lb) * _sigmoid(hf)
    logg = jnp.log(fgate)
    hk_ref[...] = 1.0 - fgate
    hq_ref[...] = hq * _sigmoid(hq)
    sg_ref[...] = hg * _sigmoid(hg)
    if chunk == seq_len and seq_len < tm:
        tri_c = tri
    else:
        tri_c = jnp.where(ct <= rt, jnp.where((ct // chunk) == (rt // chunk), 1.0, 0.0), 0.0)
    bc_ref[...] = _dot_exact(tri_c, logg)


def _even_in_op(x, sh, sc, nw, w_in, fb, qnw, knw, lb, *, seq_len, chunk, layer_e, attn_operands):
    n, d = x.shape
    tm = min(256, n)
    sh_a, mod_spec = _mod_operand(sh, seq_len, tm)
    sc_a, _ = _mod_operand(sc, seq_len, tm)
    wcols = w_in.shape[1]
    const = lambda shape: pl.BlockSpec(shape, lambda i: (0,) * len(shape))
    tok = lambda w: pl.BlockSpec((tm, w), lambda i: (i, 0))
    out_w = [FOX_W, FOX_W, FOX_W, LANES, LANES, HG_W, HG_W, HG_W, HG_W, HG_W]
    out_dt = [F32] * len(out_w)
    layout = _fox_operand_layout() if attn_operands else ()
    if attn_operands:
        out_w += [FOX_HEADS * LANES, FOX_HEADS * LANES, FOX_W]
        out_dt += [BF16, BF16, BF16]
    return pl.pallas_call(
        functools.partial(_even_in_body, seq_len=seq_len, chunk=chunk, layer_e=layer_e, attn_operands=attn_operands),
        grid=(n // tm,),
        in_specs=[tok(d), mod_spec, mod_spec, const((1, d)), const((d, wcols)), const((1, LANES)),
                  const((1, FOX_W)), const((1, FOX_W)), const(lb.shape)] + [const(a.shape) for a in layout],
        out_specs=[tok(w) for w in out_w],
        out_shape=[jax.ShapeDtypeStruct((n, w), dt) for w, dt in zip(out_w, out_dt)],
        scratch_shapes=[pltpu.VMEM((1, LANES), F32)],
        compiler_params=_cparams("arbitrary"),
        name="even_in",
    )(x, sh_a, sc_a, nw, w_in, fb, qnw, knw, lb, *layout)


def _fox_prompt_body(qa_ref, ka_ref, v_ref, o_ref, m_sc, l_sc, acc_sc):
    qi = pl.program_id(1)
    ki = pl.program_id(2)
    tq = qa_ref.shape[0]
    tk = ka_ref.shape[0]

    @pl.when(ki == 0)
    def _():
        m_sc[...] = jnp.full_like(m_sc, -jnp.inf)
        l_sc[...] = jnp.zeros_like(l_sc)
        acc_sc[...] = jnp.zeros_like(acc_sc)

    def accumulate(diagonal):
        if diagonal:
            causal = _iota((tq, tk), 1) <= _iota((tq, tk), 0)
        for h in range(FOX_HEADS):
            blk = slice(h * LANES, (h + 1) * LANES)
            pair = slice((h // 2) * LANES, (h // 2 + 1) * LANES)
            s = _dot_nt(qa_ref[:, blk], ka_ref[:, blk])
            if diagonal:
                s = jnp.where(causal, s, MASK_VALUE)
            m_old = m_sc[h]
            m_new = jnp.maximum(m_old, jnp.max(s, axis=-1, keepdims=True))
            alpha = jnp.exp(m_old - m_new)
            p = jnp.exp(s - pltpu.repeat(m_new, tk // LANES, axis=1))
            l_sc[h] = alpha * l_sc[h] + jnp.sum(p, axis=-1, keepdims=True)
            m_sc[h] = m_new
            acc_sc[h] = alpha * acc_sc[h] + _dot(p.astype(BF16), v_ref[:, pair])

    @pl.when(ki < qi)
    def _():
        accumulate(False)

    @pl.when(ki == qi)
    def _():
        accumulate(True)
        lane_head = _iota((tq, LANES), 1) // FOX_HEAD_DIM
        for pair in range(FOX_HEADS // 2):
            o0 = acc_sc[2 * pair] / l_sc[2 * pair]
            o1 = acc_sc[2 * pair + 1] / l_sc[2 * pair + 1]
            o_ref[:, pair * LANES:(pair + 1) * LANES] = jnp.where(lane_head == 0, o0, o1)


def _fox_prompt_op(qa, ka, vb, *, n_seq, seq_len):
    n = qa.shape[0]
    t = min(512, seq_len)
    nt = seq_len // t
    qmap = lambda b, qi, ki: (b * nt + qi, 0)
    kmap = lambda b, qi, ki: (b * nt + jnp.minimum(ki, qi), 0)
    stat = pltpu.VMEM((FOX_HEADS, t, LANES), F32)
    return pl.pallas_call(
        _fox_prompt_body,
        grid=(n_seq, nt, nt),
        in_specs=[pl.BlockSpec((t, FOX_HEADS * LANES), qmap), pl.BlockSpec((t, FOX_HEADS * LANES), kmap),
                  pl.BlockSpec((t, FOX_W), kmap)],
        out_specs=pl.BlockSpec((t, FOX_W), qmap),
        out_shape=jax.ShapeDtypeStruct((n, FOX_W), F32),
        scratch_shapes=[stat, stat, stat],
        compiler_params=_cparams("parallel", "parallel", "arbitrary"),
        name="fox_prompt",
    )(qa, ka, vb)


def _fox_sample_body(pt_ref, q_ref, kn_ref, vn_ref, fq_ref, fkn_ref, *rest, n_pages, t_new, scale):
    k_refs = rest[:n_pages]
    v_refs = rest[n_pages:2 * n_pages]
    lp_refs = rest[2 * n_pages:3 * n_pages]
    o_ref = rest[3 * n_pages]
    rows = FOX_HEADS * t_new
    row_head = _iota((rows, FOX_W), 0) // t_new
    col_head = _iota((rows, FOX_W), 1) // FOX_HEAD_DIM
    qbd = jnp.where(row_head == col_head, jnp.concatenate([q_ref[...]] * FOX_HEADS, axis=0), 0.0).astype(BF16)
    fq = fq_ref[:, 0:1]
    lp = jnp.concatenate([lp_refs[j][...] for j in range(n_pages)], axis=0)
    n_lp = n_pages * FOX_HEADS
    later_key = (_iota((PAGE_SIZE, PAGE_SIZE), 0) > _iota((PAGE_SIZE, PAGE_SIZE), 1)).astype(F32)
    r, c = _iota((n_lp, n_lp), 0), _iota((n_lp, n_lp), 1)
    later_page = jnp.where((c % FOX_HEADS) == (r % FOX_HEADS), jnp.where((c // FOX_HEADS) > (r // FOX_HEADS), 1.0, 0.0), 0.0)
    page_total = jnp.broadcast_to(jnp.sum(lp, axis=-1, keepdims=True), (n_lp, PAGE_SIZE))
    suf = _dot_exact(lp, later_key) + _dot_exact(later_page, page_total)
    bias = jnp.concatenate(
        [jnp.concatenate([jnp.broadcast_to(suf[j * FOX_HEADS + h:j * FOX_HEADS + h + 1, :], (t_new, PAGE_SIZE))
                          for h in range(FOX_HEADS)], axis=0) for j in range(n_pages)], axis=1)
    s_past = jnp.concatenate([_dot_nt(qbd, k_refs[j][...].astype(BF16)) for j in range(n_pages)], axis=1)
    s_past = s_past * scale + fq + bias
    pad = jnp.zeros((PAGE_SIZE - t_new, FOX_W), F32)
    kn = jnp.concatenate([kn_ref[...], pad], axis=0).astype(BF16)
    vn = jnp.concatenate([vn_ref[...], pad], axis=0).astype(BF16)
    s_new = _dot_nt(qbd, kn) * scale + fq - fkn_ref[...]
    key = _iota((rows, PAGE_SIZE), 1)
    tok = _iota((rows, PAGE_SIZE), 0) % t_new
    s_new = jnp.where(key <= tok, s_new, MASK_VALUE)
    m = jnp.maximum(jnp.max(s_past, axis=-1, keepdims=True), jnp.max(s_new, axis=-1, keepdims=True))
    p_past = jnp.exp(s_past - m)
    p_new = jnp.exp(s_new - m)
    denom = jnp.sum(p_past, axis=-1, keepdims=True) + jnp.sum(p_new, axis=-1, keepdims=True)
    o = _dot(p_new.astype(BF16), vn)
    for j in range(n_pages):
        o = o + _dot(p_past[:, j * PAGE_SIZE:(j + 1) * PAGE_SIZE].astype(BF16), v_refs[j][...].astype(BF16))
    o = o / denom
    out_head = _iota((t_new, FOX_W), 1) // FOX_HEAD_DIM
    out = jnp.zeros((t_new, FOX_W), F32)
    for h in range(FOX_HEADS):
        out = out + jnp.where(out_head == h, o[h * t_new:(h + 1) * t_new, :], 0.0)
    o_ref[...] = out


def _fox_sample_op(page_table, q, k_new, v_new, fq_rows, fk_new_rows, cache_k, cache_v, logf_t, layer_e, t_new):
    n_seq, n_pages = page_table.shape
    rows = FOX_HEADS * t_new
    seq = lambda w: pl.BlockSpec((t_new, w), lambda b, pt: (b, 0))
    paged = lambda r, w: [pl.BlockSpec((None, None, r, w), functools.partial(
        lambda b, pt, j: (layer_e, pt[b, j], 0, 0), j=j)) for j in range(n_pages)]
    grid_spec = pltpu.PrefetchScalarGridSpec(
        num_scalar_prefetch=1,
        grid=(n_seq,),
        in_specs=[seq(FOX_W), seq(FOX_W), seq(FOX_W),
                  pl.BlockSpec((None, rows, LANES), lambda b, pt: (b, 0, 0)),
                  pl.BlockSpec((None, rows, LANES), lambda b, pt: (b, 0, 0))]
        + paged(PAGE_SIZE, FOX_W) + paged(PAGE_SIZE, FOX_W) + paged(FOX_HEADS, PAGE_SIZE),
        out_specs=seq(FOX_W),
    )
    return pl.pallas_call(
        functools.partial(_fox_sample_body, n_pages=n_pages, t_new=t_new, scale=FOX_HEAD_DIM ** -0.5),
        grid_spec=grid_spec,
        out_shape=jax.ShapeDtypeStruct(q.shape, F32),
        compiler_params=_cparams("parallel"),
        name="fox_sample",
    )(page_table, q, k_new, v_new, fq_rows, fk_new_rows,
      *([cache_k] * n_pages), *([cache_v] * n_pages), *([logf_t] * n_pages))


def _hgrn_chunk(qc, kc, vc, bcc, state):
    n_rows = qc.shape[0]
    last = bcc[n_rows - 1:n_rows, :]
    o_inter = _dot((qc * jnp.exp(bcc)).astype(BF16), state.astype(BF16))
    kv = _dot_tn((kc * jnp.exp(last - bcc)).astype(BF16), vc.astype(BF16))
    decay_col = jnp.broadcast_to(jnp.exp(last), (HG_DK, HG_DK)).T
    return o_inter, decay_col * state + kv


def _hgrn_diag(q, k, v, bc, chunk):
    rows = q.shape[0]
    n = rows // chunk
    q3, k3, v3, b3 = (a.reshape(n, chunk, HG_DK) for a in (q, k, v, bc))
    sidx = _iota((n, chunk, HG_DK), 1)
    outs = []
    for t in range(chunk):
        decay = jnp.exp(jnp.where(sidx <= t, b3[:, t:t + 1, :] - b3, -jnp.inf))
        att = jnp.sum(decay * q3[:, t:t + 1, :] * k3, axis=-1, keepdims=True)
        outs.append(jnp.sum(att * v3, axis=1, keepdims=True))
    return jnp.concatenate(outs, axis=1).reshape(rows, HG_DK)


def _hgrn_finish(o, onw, sg):
    ms = jnp.mean(o * o, axis=-1, keepdims=True)
    return o * lax.rsqrt(ms + EPS) * onw * sg


def _hgrn_prompt_body(q_ref, k_ref, v_ref, bc_ref, sg_ref, onw_ref, o_ref, st_ref, s_sc, oi_sc, *, chunk):
    i = pl.program_id(0)
    n_seq, tm, _ = q_ref.shape

    @pl.when(i == 0)
    def _():
        s_sc[...] = jnp.zeros_like(s_sc)

    def chunk_body(c, carry):
        r0 = pl.multiple_of(c * chunk, chunk)
        for b in range(n_seq):
            for h in range(HG_HEADS):
                cols = slice(h * HG_DK, (h + 1) * HG_DK)
                o_inter, s_new = _hgrn_chunk(q_ref[b, pl.ds(r0, chunk), cols], k_ref[b, pl.ds(r0, chunk), cols],
                                             v_ref[b, pl.ds(r0, chunk), cols], bc_ref[b, pl.ds(r0, chunk), cols],
                                             s_sc[b * HG_HEADS + h])
                s_sc[b * HG_HEADS + h] = s_new
                oi_sc[b, pl.ds(r0, chunk), cols] = o_inter
        return carry

    lax.fori_loop(0, tm // chunk, chunk_body, 0)
    for b in range(n_seq):
        for h in range(HG_HEADS):
            cols = slice(h * HG_DK, (h + 1) * HG_DK)
            o = oi_sc[b, :, cols] + _hgrn_diag(q_ref[b, :, cols], k_ref[b, :, cols], v_ref[b, :, cols],
                                               bc_ref[b, :, cols], chunk)
            o_ref[b, :, cols] = _hgrn_finish(o, onw_ref[...], sg_ref[b, :, cols])
            st_ref[b, h] = s_sc[b * HG_HEADS + h]


def _hgrn_prompt_op(hq, hk, hv, bc, sg, onw, *, n_seq, seq_len, chunk):
    tm = min(512, seq_len)
    view = lambda a: a.reshape(n_seq, seq_len, HG_W)
    tok = pl.BlockSpec((n_seq, tm, HG_W), lambda i: (0, i, 0))
    o, st = pl.pallas_call(
        functools.partial(_hgrn_prompt_body, chunk=chunk),
        grid=(seq_len // tm,),
        in_specs=[tok] * 5 + [pl.BlockSpec((1, HG_DK), lambda i: (0, 0))],
        out_specs=[tok, pl.BlockSpec((n_seq, HG_HEADS, HG_DK, HG_DK), lambda i: (0, 0, 0, 0))],
        out_shape=[jax.ShapeDtypeStruct((n_seq, seq_len, HG_W), F32),
                   jax.ShapeDtypeStruct((n_seq, HG_HEADS, HG_DK, HG_DK), F32)],
        scratch_shapes=[pltpu.VMEM((n_seq * HG_HEADS, HG_DK, HG_DK), F32), pltpu.VMEM((n_seq, tm, HG_W), F32)],
        compiler_params=_cparams("arbitrary"),
        name="hgrn_prompt",
    )(view(hq), view(hk), view(hv), view(bc), view(sg), onw)
    return o.reshape(n_seq * seq_len, HG_W), st


def _hgrn_sample_body(q_ref, k_ref, v_ref, bc_ref, sg_ref, onw_ref, s0_ref, o_ref, st_ref, *, t_new):
    n_b = s0_ref.shape[0]
    for h in range(HG_HEADS):
        cols = slice(h * HG_DK, (h + 1) * HG_DK)
        parts = []
        for b in range(n_b):
            rows = slice(b * t_new, (b + 1) * t_new)
            o_inter, s_new = _hgrn_chunk(q_ref[rows, cols], k_ref[rows, cols], v_ref[rows, cols], bc_ref[rows, cols],
                                         s0_ref[b, h])
            st_ref[b, h] = s_new
            parts.append(o_inter)
        o = jnp.concatenate(parts, axis=0) + _hgrn_diag(q_ref[:, cols], k_ref[:, cols], v_ref[:, cols],
                                                        bc_ref[:, cols], t_new)
        o_ref[:, cols] = _hgrn_finish(o, onw_ref[...], sg_ref[:, cols])


def _hgrn_sample_op(hq, hk, hv, bc, sg, onw, s0, *, t_new):
    n_seq = s0.shape[0]
    n_b = min(8, n_seq)
    tok = pl.BlockSpec((n_b * t_new, HG_W), lambda i: (i, 0))
    st_spec = pl.BlockSpec((n_b, HG_HEADS, HG_DK, HG_DK), lambda i: (i, 0, 0, 0))
    return pl.pallas_call(
        functools.partial(_hgrn_sample_body, t_new=t_new),
        grid=(n_seq // n_b,),
        in_specs=[tok] * 5 + [pl.BlockSpec((1, HG_DK), lambda i: (0, 0)), st_spec],
        out_specs=[tok, st_spec],
        out_shape=[jax.ShapeDtypeStruct(hq.shape, F32), jax.ShapeDtypeStruct(s0.shape, F32)],
        compiler_params=_cparams("parallel"),
        name="hgrn_sample",
    )(hq, hk, hv, bc, sg, onw, s0)


def _residual_and_norm2(x, mix, g1, nw2, sh2, sc2, x1_ref, h2_ref):
    x1 = x + g1 * mix
    x1_ref[...] = x1
    h2_ref[...] = _modulate(x1, nw2, sh2, sc2).astype(BF16)


def _even_out_body(of_ref, oh_ref, x_ref, g1_ref, sh2_ref, sc2_ref, nw2_ref, wa_ref, wb_ref, x1_ref, h2_ref):
    mix = _dot(of_ref[...].astype(BF16), wa_ref[...]) + _dot(oh_ref[...].astype(BF16), wb_ref[...])
    _residual_and_norm2(x_ref[...], mix, g1_ref[...], nw2_ref[...], sh2_ref[...], sc2_ref[...], x1_ref, h2_ref)


def _odd_out_body(y_ref, x_ref, g1_ref, sh2_ref, sc2_ref, nw2_ref, w_ref, b_ref, x1_ref, h2_ref):
    d = x_ref.shape[1]
    ab = _dot(_gelu(y_ref[...]).astype(BF16), w_ref[...]) + b_ref[...]
    mix = ab[:, :d] * _sigmoid(ab[:, d:])
    _residual_and_norm2(x_ref[...], mix, g1_ref[...], nw2_ref[...], sh2_ref[...], sc2_ref[...], x1_ref, h2_ref)


def _mixer_out_op(body, name, token_inputs, x, g1, sh2, sc2, nw2, consts, *, seq_len):
    n, d = x.shape
    tm = min(512, n)
    mods = [_mod_operand(v, seq_len, tm) for v in (g1, sh2, sc2)]
    tok = lambda a: pl.BlockSpec((tm, a.shape[1]), lambda i: (i, 0))
    const = lambda a: pl.BlockSpec(a.shape, lambda i: (0,) * a.ndim)
    return pl.pallas_call(
        body,
        grid=(n // tm,),
        in_specs=[tok(a) for a in token_inputs] + [tok(x)] + [m[1] for m in mods] + [const(nw2)] + [const(a) for a in consts],
        out_specs=[tok(x), tok(x)],
        out_shape=[jax.ShapeDtypeStruct((n, d), F32), jax.ShapeDtypeStruct((n, d), BF16)],
        compiler_params=_cparams("parallel"),
        name=name,
    )(*token_inputs, x, *[m[0] for m in mods], nw2, *consts)


def _s5_prep_body(ar_ref, ai_ref, ls_ref, brt_ref, bit_ref, pwr_ref, pwi_ref, bbr_ref, bbi_ref):
    ar, ai = ar_ref[...], ai_ref[...]
    dt = jnp.exp(ls_ref[...])
    mag = jnp.exp(ar * dt)
    ang = ai * dt
    abr, abi = mag * jnp.cos(ang), mag * jnp.sin(ang)
    nr, ni = abr - 1.0, abi
    den = ar * ar + ai * ai
    cr = (nr * ar + ni * ai) / den
    ci = (ni * ar - nr * ai) / den
    bbr_ref[...] = cr * brt_ref[...] - ci * bit_ref[...]
    bbi_ref[...] = cr * bit_ref[...] + ci * brt_ref[...]
    pr, pi = abr, abi
    for k in range(pwr_ref.shape[0]):
        pwr_ref[k:k + 1, :] = pr
        pwi_ref[k:k + 1, :] = pi
        pr, pi = pr * abr - pi * abi, pr * abi + pi * abr


def _s5_prep_op(a_re, a_im, log_step, b_re, b_im, n_pow):
    groups, n_state = a_re.shape
    w = groups * n_state
    flat = lambda a: a.reshape(1, w)
    ls = flat(jnp.broadcast_to(log_step[:, None], (groups, n_state)))
    bt = lambda b: jnp.transpose(b, (2, 0, 1)).reshape(S5_GROUP, w)
    return pl.pallas_call(
        _s5_prep_body,
        out_shape=[jax.ShapeDtypeStruct((n_pow, w), F32), jax.ShapeDtypeStruct((n_pow, w), F32),
                   jax.ShapeDtypeStruct((S5_GROUP, w), F32), jax.ShapeDtypeStruct((S5_GROUP, w), F32)],
        name="s5_prep",
    )(flat(a_re), flat(a_im), ls, bt(b_re), bt(b_im))


def _s5_body(x_ref, sh_ref, sc_ref, nw_ref, bre_ref, bim_ref, cre_ref, cim_ref, d_ref, pwr_ref, pwi_ref,
             x0r_ref, x0i_ref, y_ref, sr_ref, si_ref, xr_sc, xi_sc, str_sc, sti_sc, car_r, car_i, *, carry):
    jt = pl.program_id(1)
    n_l, n_j, d = x_ref.shape
    rows = n_l * n_j
    n_tiles = d // LANES
    sw = xr_sc.shape[1] // n_tiles
    u = _modulate(x_ref[...], nw_ref[...], sh_ref[...], sc_ref[...]).reshape(rows, d)
    ub = u.astype(BF16)
    for r in range(n_tiles):
        xr_sc[:, r * sw:(r + 1) * sw] = _dot(ub[:, r * LANES:(r + 1) * LANES], bre_ref[r])
        xi_sc[:, r * sw:(r + 1) * sw] = _dot(ub[:, r * LANES:(r + 1) * LANES], bim_ref[r])
    ar, ai = pwr_ref[0:1, :], pwi_ref[0:1, :]
    for tl in range(1, n_l):
        prev = slice((tl - 1) * n_j, tl * n_j)
        cur = slice(tl * n_j, (tl + 1) * n_j)
        pr, pi = xr_sc[prev, :], xi_sc[prev, :]
        xr_sc[cur, :] = ar * pr - ai * pi + xr_sc[cur, :]
        xi_sc[cur, :] = ar * pi + ai * pr + xi_sc[cur, :]
    if carry:
        @pl.when(jt == 0)
        def _():
            car_r[...] = jnp.zeros_like(car_r)
            car_i[...] = jnp.zeros_like(car_i)

        plr, pli = pwr_ref[n_l - 1:n_l, :], pwi_ref[n_l - 1:n_l, :]
        last0 = (n_l - 1) * n_j

        def chunk_body(j, c):
            cr, ci = car_r[...], car_i[...]
            str_sc[pl.ds(j, 1), :] = cr
            sti_sc[pl.ds(j, 1), :] = ci
            car_r[...] = plr * cr - pli * ci + xr_sc[pl.ds(last0 + j, 1), :]
            car_i[...] = plr * ci + pli * cr + xi_sc[pl.ds(last0 + j, 1), :]
            return c

        lax.fori_loop(0, n_j, chunk_body, 0)
        sr_ref[...] = car_r[...]
        si_ref[...] = car_i[...]
    else:
        str_sc[...] = x0r_ref[...]
        sti_sc[...] = x0i_ref[...]
    s0r, s0i = str_sc[...], sti_sc[...]
    for tl in range(n_l):
        cur = slice(tl * n_j, (tl + 1) * n_j)
        pr, pi = pwr_ref[tl:tl + 1, :], pwi_ref[tl:tl + 1, :]
        xr_sc[cur, :] = xr_sc[cur, :] + (pr * s0r - pi * s0i)
        xi_sc[cur, :] = xi_sc[cur, :] + (pr * s0i + pi * s0r)
    if not carry:
        last = slice((n_l - 1) * n_j, n_l * n_j)
        sr_ref[...] = xr_sc[last, :]
        si_ref[...] = xi_sc[last, :]
    for r in range(n_tiles):
        cols = slice(r * LANES, (r + 1) * LANES)
        y = _dot(xr_sc[:, r * sw:(r + 1) * sw].astype(BF16), cre_ref[r]) \
            - _dot(xi_sc[:, r * sw:(r + 1) * sw].astype(BF16), cim_ref[r])
        y = y + d_ref[:, cols] * u[:, cols]
        y_ref[:, :, cols] = y.reshape(n_l, n_j, LANES)


def _s5_op(x, sh, sc, nw, mats, pw_r, pw_i, x0, *, n_seq, seq_len, chunk):
    n, d = x.shape
    bre, bim, cre, cim, dsk = mats
    w = pw_r.shape[1]
    carry = x0 is None
    if carry:
        n_b, n_j_all = n_seq, seq_len // chunk
    else:
        assert seq_len == chunk
        n_b, n_j_all = 1, n_seq
    xp = jnp.transpose(x.reshape(n_b, n_j_all, chunk, d), (0, 2, 1, 3))
    n_j = min(512 // chunk, n_j_all)
    if carry:
        mod = lambda v: (v.reshape(n_seq, 1, d), pl.BlockSpec((None, 1, d), lambda b, j: (b, 0, 0)))
        x0r = x0i = jnp.zeros((8, LANES), F32)
        x0_spec = pl.BlockSpec((8, LANES), lambda b, j: (0, 0))
        st_shape = jax.ShapeDtypeStruct((n_b, 1, w), F32)
        st_spec = pl.BlockSpec((None, 1, w), lambda b, j: (b, 0, 0))
    else:
        mod = lambda v: (v.reshape(n_j_all // n_j, n_j, d), pl.BlockSpec((None, n_j, d), lambda b, j: (j, 0, 0)))
        x0r, x0i = x0
        x0_spec = pl.BlockSpec((n_j, w), lambda b, j: (j, 0))
        st_shape = jax.ShapeDtypeStruct((n_seq, w), F32)
        st_spec = pl.BlockSpec((n_j, w), lambda b, j: (j, 0))
    (sh_a, mod_spec), (sc_a, _) = mod(sh), mod(sc)
    const = lambda a: pl.BlockSpec(a.shape, lambda b, j: (0,) * a.ndim)
    xspec = pl.BlockSpec((None, chunk, n_j, d), lambda b, j: (b, 0, j, 0))
    rows = chunk * n_j
    y, sr, si = pl.pallas_call(
        functools.partial(_s5_body, carry=carry),
        grid=(n_b, n_j_all // n_j),
        in_specs=[xspec, mod_spec, mod_spec, const(nw), const(bre), const(bim), const(cre), const(cim), const(dsk),
                  const(pw_r), const(pw_i), x0_spec, x0_spec],
        out_specs=[xspec, st_spec, st_spec],
        out_shape=[jax.ShapeDtypeStruct(xp.shape, F32), st_shape, st_shape],
        scratch_shapes=[pltpu.VMEM((rows, w), F32), pltpu.VMEM((rows, w), F32),
                        pltpu.VMEM((n_j, w), F32), pltpu.VMEM((n_j, w), F32),
                        pltpu.VMEM((1, w), F32), pltpu.VMEM((1, w), F32)],
        compiler_params=_cparams("parallel", "arbitrary"),
        name="s5_scan",
    )(xp, sh_a, sc_a, nw, bre, bim, cre, cim, dsk, pw_r, pw_i, x0r, x0i)
    y = jnp.transpose(y, (0, 2, 1, 3)).reshape(n, d)
    return y, sr.reshape(-1, w), si.reshape(-1, w)


def _s5_matrices(bbr_t, bbi_t, c_re, c_im, d_skip):
    groups = c_re.shape[0]
    per_tile = LANES // S5_GROUP
    n_tiles = groups // per_tile
    eye = jnp.eye(per_tile, dtype=F32)

    def b_mat(bt):
        b4 = jnp.transpose(bt.reshape(S5_GROUP, n_tiles, per_tile, S5_STATE), (1, 2, 0, 3))
        m = b4[:, :, :, None, :] * eye[None, :, None, :, None]
        return m.reshape(n_tiles, LANES, per_tile * S5_STATE).astype(BF16)

    def c_mat(c):
        c4 = jnp.transpose(c.reshape(n_tiles, per_tile, S5_GROUP, S5_STATE), (0, 3, 1, 2))
        m = c4[:, None, :, :, :] * eye[None, :, None, :, None]
        return m.reshape(n_tiles, per_tile * S5_STATE, LANES).astype(BF16)

    return b_mat(bbr_t), b_mat(bbi_t), c_mat(c_re), c_mat(c_im), d_skip.reshape(1, groups * S5_GROUP)


def _top_rows(s, count):
    out = []
    for _ in range(count):
        m = jnp.max(s, axis=0, keepdims=True)
        out.append(m)
        s = jnp.where(s == m, -jnp.inf, s)
    return jnp.concatenate(out, axis=0)


def _top_rows_ranked(s, count):
    rank = jnp.full(s.shape, RANK_NONE, F32)
    out = []
    for k in range(count):
        m = jnp.max(s, axis=0, keepdims=True)
        hit = s == m
        rank = jnp.where(hit, float(k), rank)
        s = jnp.where(hit, -jnp.inf, s)
        out.append(m)
    return jnp.concatenate(out, axis=0), rank


def _peer_route(s1, s2):
    k = PEER_TOPK
    sub = 8
    v1 = _top_rows(s1, k)
    v2, rank2 = _top_rows_ranked(s2, k)
    row = _iota((sub, s1.shape[1]), 0)
    blocks = [v1[0:1, :] + v2]
    for a in range(1, sub):
        blocks.append(jnp.where(row < k // (a + 1), v1[a:a + 1, :] + v2[0:sub, :], -jnp.inf))
    blocks.append(v1[sub:k, :] + v2[0:1, :])
    cand = jnp.concatenate(blocks, axis=0)
    kth = _top_rows(cand, k)[k - 1:k, :]
    z = jnp.sum(jnp.where(cand >= kth, jnp.exp(cand - (v1[0:1, :] + v2[0:1, :])), 0.0), axis=0, keepdims=True)
    count = jnp.zeros_like(s1)
    for a in range(k):
        if a < sub:
            n_a = jnp.sum(jnp.where(blocks[a] >= kth, 1.0, 0.0), axis=0, keepdims=True)
        else:
            n_a = jnp.where(blocks[sub][a - sub:a - sub + 1, :] >= kth, 1.0, 0.0)
        count = count + jnp.where(s1 == v1[a:a + 1, :], n_a, 0.0)
    return count, jnp.exp(s1 - v1[0:1, :]) / z, rank2, jnp.exp(s2 - v2[0:1, :])


def _peer_body(h2_ref, x1_ref, g2_ref, wq_ref, keys_ref, u_ref, vt_ref, o_ref,
               rank2_sc, e2_sc, cnt_sc, e1_sc, w_even, w_odd, acc_sc, *, n_chunks_static):
    c = pl.program_id(1)
    tm = h2_ref.shape[0]
    e_c = u_ref.shape[0]
    per_chunk = e_c // PEER_NKEYS
    tw = min(PEER_TOKEN_SPLIT, tm)

    @pl.when(c == 0)
    def _():
        acc_sc[...] = jnp.zeros_like(acc_sc)
        q = _dot(h2_ref[...], wq_ref[...]).astype(BF16)
        for h in range(PEER_HEADS):
            e1_sc[h] = _dot_nt(keys_ref[0], q[:, (2 * h) * LANES:(2 * h + 1) * LANES])
            cnt_sc[h] = _dot_nt(keys_ref[1], q[:, (2 * h + 1) * LANES:(2 * h + 2) * LANES])

        def route_head(h, carry):
            for col in range(tm // LANES):
                ts = slice(col * LANES, (col + 1) * LANES)
                count, e1, rank2, e2 = _peer_route(e1_sc[h, :, ts], cnt_sc[h, :, ts])
                cnt_sc[h, :, ts] = count
                e1_sc[h, :, ts] = e1
                rank2_sc[h, :, ts] = rank2.astype(BF16)
                e2_sc[h, :, ts] = e2.astype(BF16)
            return carry

        lax.fori_loop(0, PEER_HEADS, route_head, 0)

    def chunk_step(parity, do_weights, do_values):
        w_cur, w_prev = w_bufs[parity], w_bufs[1 - parity]
        if do_weights:
            for t0 in range(0, tm, tw):
                ts = slice(t0, t0 + tw)
                act = _dot_nt(u_ref[...], h2_ref[ts, :])
                for il in range(per_chunk):
                    rows = slice(il * PEER_NKEYS, (il + 1) * PEER_NKEYS)
                    i = c * per_chunk + il
                    gate = jnp.zeros((PEER_NKEYS, tw), BF16)
                    for h in range(PEER_HEADS):
                        n_sel = cnt_sc[h, pl.ds(i, 1), ts].astype(BF16)
                        e1 = e1_sc[h, pl.ds(i, 1), ts].astype(BF16)
                        gate = gate + jnp.where(rank2_sc[h, :, ts] < n_sel, e2_sc[h, :, ts], jnp.zeros((), BF16)) * e1
                    w_cur[rows, ts] = gate * _gelu(act[rows, :]).astype(BF16)
        if do_values:
            acc_sc[...] += _dot(vt_ref[...], w_prev[...])

    w_bufs = (w_even, w_odd)
    last = n_chunks_static

    @pl.when(c == 0)
    def _():
        chunk_step(0, True, False)

    for parity in (0, 1):
        @pl.when(jnp.logical_and(jnp.logical_and(c >= 1, c < last), c % 2 == parity))
        def _(parity=parity):
            chunk_step(parity, True, True)

    @pl.when(c == last)
    def _():
        chunk_step(last % 2, False, True)
        o_ref[...] = x1_ref[...] + g2_ref[...] * acc_sc[...].T


def _peer_op(h2, x1, g2, wq, keys, u, vt, *, seq_len):
    n, d = x1.shape
    tm = min(512, n)
    e_c = PEER_EXPERT_CHUNK
    n_chunks = u.shape[0] // e_c
    g2_a, g2_spec = _mod_operand(g2, seq_len, tm, grid_rank=2)
    const = lambda a: pl.BlockSpec(a.shape, lambda i, c: (0,) * a.ndim)
    tok = lambda a: pl.BlockSpec((tm, a.shape[1]), lambda i, c: (i, 0))
    head_f32 = pltpu.VMEM((PEER_HEADS, PEER_NKEYS, tm), F32)
    head_bf16 = pltpu.VMEM((PEER_HEADS, PEER_NKEYS, tm), BF16)
    return pl.pallas_call(
        functools.partial(_peer_body, n_chunks_static=n_chunks),
        grid=(n // tm, n_chunks + 1),
        in_specs=[tok(h2), tok(x1), g2_spec, const(wq), const(keys),
                  pl.BlockSpec((e_c, d), lambda i, c: (jnp.minimum(c, n_chunks - 1), 0)),
                  pl.BlockSpec((d, e_c), lambda i, c: (0, jnp.maximum(c - 1, 0)))],
        out_specs=tok(x1),
        out_shape=jax.ShapeDtypeStruct((n, d), F32),
        scratch_shapes=[head_bf16, head_bf16, head_f32, head_f32,
                        pltpu.VMEM((e_c, tm), BF16), pltpu.VMEM((e_c, tm), BF16), pltpu.VMEM((d, tm), F32)],
        compiler_params=_cparams("parallel", "arbitrary"),
        name="peer",
    )(h2, x1, g2_a, wq, keys, u, vt)


def _trunk(x, mods, past, wts, *, n_seq, seq_len, chunk):
    depth = len(mods)
    ks, vs, lfs, hs, srs, sis = [], [], [], [], [], []
    for l in range(depth):
        sh1, sc1, g1, sh2, sc2, g2 = mods[l]
        nw1, nw2 = wts["norm1_w"][l:l + 1], wts["norm2_w"][l:l + 1]
        if l % 2 == 0:
            e = l // 2
            q, k, v, lf, cum, hq, hk, hv, bc, sg, *attn = _even_in_op(
                x, sh1, sc1, nw1, wts["w_in"][e], wts["fbias"][e], wts["qnw"][e], wts["knw"][e], wts["hgrn_lb"],
                seq_len=seq_len, chunk=chunk, layer_e=e, attn_operands=past is None)
            if past is None:
                o_fox = _fox_prompt_op(*attn, n_seq=n_seq, seq_len=seq_len)
                o_h, s_new = _hgrn_prompt_op(hq, hk, hv, bc, sg, wts["onw"][e], n_seq=n_seq, seq_len=seq_len, chunk=chunk)
            else:
                cum8 = cum[:, :FOX_HEADS].reshape(n_seq, seq_len, FOX_HEADS)
                per_head = jnp.transpose(cum8, (0, 2, 1))
                rows = FOX_HEADS * seq_len
                fq_rows = jnp.broadcast_to(per_head.reshape(n_seq, rows, 1), (n_seq, rows, LANES))
                fk_new = jnp.pad(jnp.repeat(per_head, seq_len, axis=1), ((0, 0), (0, 0), (0, LANES - seq_len)))
                o_fox = _fox_sample_op(past["page_table"], q, k, v, fq_rows, fk_new,
                                       past["cache_k"], past["cache_v"], past["logf_t"], e, seq_len)
                o_h, s_new = _hgrn_sample_op(hq, hk, hv, bc, sg, wts["onw"][e], past["state_hgrn"][e], t_new=seq_len)
            ks.append(k)
            vs.append(v)
            lfs.append(lf[:, :FOX_HEADS])
            hs.append(s_new)
            x1, h2 = _mixer_out_op(_even_out_body, "even_out", [o_fox, o_h], x, g1, sh2, sc2, nw2,
                                   [wts["w_out_a"][e], wts["w_out_b"][e]], seq_len=seq_len)
        else:
            o = l // 2
            x0 = None if past is None else (past["state_ssm_re"][o], past["state_ssm_im"][o])
            y, sr, si = _s5_op(x, sh1, sc1, nw1, wts["s5_mats"][o], wts["s5_pw_r"][o], wts["s5_pw_i"][o], x0,
                               n_seq=n_seq, seq_len=seq_len, chunk=chunk)
            srs.append(sr)
            sis.append(si)
            x1, h2 = _mixer_out_op(_odd_out_body, "odd_out", [y], x, g1, sh2, sc2, nw2,
                                   [wts["w_glu"][o], wts["b_glu"][o]], seq_len=seq_len)
        x = _peer_op(h2, x1, g2, wts["peer_wq"][l], wts["peer_keys"][l], wts["peer_u"][l], wts["peer_vt"][l],
                     seq_len=seq_len)
    return x, ks, vs, lfs, hs, srs, sis


def kernel(x_prompt, x_sample, cache_k, cache_v, cache_logf, state_hgrn, state_ssm_re, state_ssm_im, page_table,
           c_prompt, c_sample, norm1_w, norm2_w, w_ada, b_ada, w_in_even, fox_fbias, q_norm_w, k_norm_w, hgrn_lb,
           hgrn_onorm_w, w_out_even, s5_A_re, s5_A_im, s5_B_re, s5_B_im, s5_C_re, s5_C_im, s5_D, s5_log_step,
           w_glu, b_glu, peer_wq, peer_subkeys, peer_u, peer_v):
    n_p_seq, t_p, d = x_prompt.shape
    n_s_seq, t_s, _ = x_sample.shape
    depth = w_ada.shape[0]
    n_even = w_in_even.shape[0]
    n_odd = w_glu.shape[0]
    n_ada = w_ada.shape[2] // d

    sizes = (FOX_W, FOX_W, FOX_W, FOX_HEADS, HG_W, HG_W, HG_W, HG_W)
    offs = [0]
    for s in sizes:
        offs.append(offs[-1] + s)
    seg = lambda i: w_in_even[:, :, offs[i]:offs[i + 1]]
    w_in = jnp.concatenate([seg(0), seg(1), seg(2), seg(4), seg(5), seg(6), seg(7),
                            jnp.pad(seg(3), ((0, 0), (0, 0), (0, LANES - FOX_HEADS)))], axis=-1).astype(BF16)
    wts = dict(
        norm1_w=norm1_w, norm2_w=norm2_w, w_in=w_in,
        fbias=jnp.pad(fox_fbias, ((0, 0), (0, LANES - FOX_HEADS))).reshape(n_even, 1, LANES),
        qnw=jnp.tile(q_norm_w, (1, FOX_HEADS)).reshape(n_even, 1, FOX_W),
        knw=jnp.tile(k_norm_w, (1, FOX_HEADS)).reshape(n_even, 1, FOX_W),
        hgrn_lb=hgrn_lb, onw=hgrn_onorm_w.reshape(n_even, 1, HG_DK),
        w_out_a=w_out_even[:, :FOX_W].astype(BF16), w_out_b=w_out_even[:, FOX_W:].astype(BF16),
        w_glu=w_glu.astype(BF16), b_glu=b_glu.reshape(n_odd, 1, 2 * d),
        peer_wq=peer_wq.astype(BF16), peer_keys=peer_subkeys.astype(BF16),
        peer_u=peer_u.astype(BF16), peer_vt=jnp.transpose(peer_v.astype(BF16), (0, 2, 1)),
    )
    mats, pws_r, pws_i = [], [], []
    for o in range(n_odd):
        pw_r, pw_i, bbr_t, bbi_t = _s5_prep_op(s5_A_re[o], s5_A_im[o], s5_log_step[o], s5_B_re[o], s5_B_im[o],
                                               max(PROMPT_CHUNK, t_s))
        mats.append(_s5_matrices(bbr_t, bbi_t, s5_C_re[o], s5_C_im[o], s5_D[o]))
        pws_r.append(pw_r)
        pws_i.append(pw_i)
    wts.update(s5_mats=mats, s5_pw_r=pws_r, s5_pw_i=pws_i)

    n_c = n_p_seq + n_s_seq
    c_all = jnp.pad(jnp.concatenate([c_prompt, c_sample], axis=0), ((0, (-n_c) % 8), (0, 0)))
    mod = _ada_op(c_all, w_ada, b_ada)
    mods_p = [[mod[l, :n_p_seq, a * d:(a + 1) * d] for a in range(n_ada)] for l in range(depth)]
    mods_s = [[mod[l, n_p_seq:n_c, a * d:(a + 1) * d] for a in range(n_ada)] for l in range(depth)]

    n_phys = cache_k.shape[1]
    past = dict(page_table=page_table,
                cache_k=cache_k.reshape(n_even, n_phys, PAGE_SIZE, FOX_W),
                cache_v=cache_v.reshape(n_even, n_phys, PAGE_SIZE, FOX_W),
                logf_t=jnp.transpose(cache_logf, (0, 1, 3, 2)),
                state_hgrn=state_hgrn,
                state_ssm_re=state_ssm_re.reshape(n_odd, n_s_seq, -1),
                state_ssm_im=state_ssm_im.reshape(n_odd, n_s_seq, -1))

    yp, kp, vp, lfp, hp, srp, sip = _trunk(x_prompt.reshape(n_p_seq * t_p, d), mods_p, None, wts,
                                           n_seq=n_p_seq, seq_len=t_p, chunk=PROMPT_CHUNK)
    ys, ksm, vsm, lfsm, hsm, srs, sis = _trunk(x_sample.reshape(n_s_seq * t_s, d), mods_s, past, wts,
                                               n_seq=n_s_seq, seq_len=t_s, chunk=t_s)

    def pack(n_seq, t, ks, vs, lfs, hs, srs_, sis_):
        return (jnp.stack(ks).reshape(n_even, n_seq, t, FOX_HEADS, FOX_HEAD_DIM),
                jnp.stack(vs).reshape(n_even, n_seq, t, FOX_HEADS, FOX_HEAD_DIM),
                jnp.stack(lfs).reshape(n_even, n_seq, t, FOX_HEADS),
                jnp.stack(hs),
                jnp.stack(srs_).reshape(n_odd, n_seq, -1, S5_STATE),
                jnp.stack(sis_).reshape(n_odd, n_seq, -1, S5_STATE))

    return ((yp.reshape(n_p_seq, t_p, d), ys.reshape(n_s_seq, t_s, d))
            + pack(n_p_seq, t_p, kp, vp, lfp, hp, srp, sip)
            + pack(n_s_seq, t_s, ksm, vsm, lfsm, hsm, srs, sis))
```
